```python
import math
import jax, jax.numpy as jnp
from jax import lax
import numpy as np


D_MODEL = 1024
BATCH = 4
SEQ = 4096
DEPTH = 1
DEC_BATCH = 128
DEC_SEQ = 8
PAST_LEN = 8192
PAGE_SIZE = 128

N_HEADS = 8
DK = D_MODEL // N_HEADS // 2
DV = 2 * DK
ATTN_W = N_HEADS * DV
LRU_BLOCKS = 8
LRU_W = ((4 * D_MODEL // 3 + 127) // 128) * 128
LRU_BW = LRU_W // LRU_BLOCKS
LRU_CONV = 4
LRU_C = 8.0
FF = 3 * D_MODEL
FFN_CONV = 3
Q_BLOCK = 128
EPS = 1e-6
IN_W = 2 * N_HEADS * 2 * DK + ATTN_W + 2 * LRU_W + 2 * D_MODEL

kernel_name = "hybrid_diffattn_rglru_convffn_step"


def rmsnorm(x, g):
    x32 = x.astype(jnp.float32)
    x32 = x32 * lax.rsqrt(jnp.mean(x32 * x32, axis=-1, keepdims=True) + EPS)
    return (x32 * g.astype(jnp.float32)).astype(x.dtype)


def alibi_slopes():
    return jnp.exp2(-8.0 * jnp.arange(1, N_HEADS + 1, dtype=jnp.float32) / N_HEADS)


def causal_dwconv(x, buf, w, b):
    k = w.shape[0]
    s = x.shape[1]
    xp = jnp.concatenate([buf.astype(x.dtype), x], axis=1)
    y = b
    for j in range(k):
        y = y + w[j] * xp[:, j:j + s]
    return y, xp[:, s:]


def split_points():
    sizes = (N_HEADS * 2 * DK, N_HEADS * 2 * DK, ATTN_W, LRU_W, LRU_W, D_MODEL)
    pts = []
    acc = 0
    for sz in sizes:
        acc += sz
        pts.append(acc)
    return pts


def diff_attn_prompt(q, k, v, lam):
    B, S = q.shape[:2]
    slopes = alibi_slopes()[:, None, None, None]
    kpos = jnp.arange(S)

    def block(qb):
        start = qb * Q_BLOCK
        qblk = lax.dynamic_slice_in_dim(q, start, Q_BLOCK, axis=1)
        s = jnp.einsum('bqhmd,bkhmd->bhmqk', qblk, k).astype(jnp.float32)
        dist = (start + jnp.arange(Q_BLOCK))[:, None] - kpos[None, :]
        s = jnp.where(dist >= 0, s - slopes * dist.astype(jnp.float32), -jnp.inf)
        p = jax.nn.softmax(s, axis=-1)
        a = p[:, :, 0] - lam * p[:, :, 1]
        return jnp.einsum('bhqk,bkhd->bqhd', a.astype(v.dtype), v)

    o = lax.map(block, jnp.arange(S // Q_BLOCK))
    return jnp.moveaxis(o, 0, 1).reshape(B, S, N_HEADS, DV)


def diff_attn_sample(q, k_new, v_new, cache_k, cache_v, page_table, layer, lam):
    Bd, T = q.shape[:2]
    n_pages = page_table.shape[1]
    past = n_pages * PAGE_SIZE
    slopes = alibi_slopes()[:, None, None, None]
    qpos = past + jnp.arange(T)

    def update(carry, s, vals):
        m, l, acc = carry
        m_new = jnp.maximum(m, s.max(-1))
        corr = jnp.exp(m - m_new)
        p = jnp.exp(s - m_new[..., None])
        l = l * corr + p.sum(-1)
        acc = acc * corr[..., None] + jnp.einsum('bhmtk,bkhd->bhmtd', p, vals.astype(jnp.float32))
        return (m_new, l, acc)

    def page_step(carry, xs):
        p_idx, phys = xs
        kp = cache_k[layer, phys].reshape(Bd, PAGE_SIZE, N_HEADS, 2, DK)
        vp = cache_v[layer, phys]
        s = jnp.einsum('bthmd,bkhmd->bhmtk', q, kp.astype(q.dtype)).astype(jnp.float32)
        dist = qpos[:, None] - (p_idx * PAGE_SIZE + jnp.arange(PAGE_SIZE))[None, :]
        s = s - slopes * dist.astype(jnp.float32)
        return update(carry, s, vp), None

    init = (jnp.full((Bd, N_HEADS, 2, T), -jnp.inf, jnp.float32),
            jnp.zeros((Bd, N_HEADS, 2, T), jnp.float32),
            jnp.zeros((Bd, N_HEADS, 2, T, DV), jnp.float32))
    carry, _ = lax.scan(page_step, init, (jnp.arange(n_pages), page_table.T))
    s = jnp.einsum('bthmd,bkhmd->bhmtk', q, k_new).astype(jnp.float32)
    dist = qpos[:, None] - qpos[None, :]
    s = jnp.where(dist >= 0, s - slopes * dist.astype(jnp.float32), -jnp.inf)
    m, l, acc = update(carry, s, v_new)
    out = acc / l[..., None]
    a = out[:, :, 0] - lam * out[:, :, 1]
    return jnp.transpose(a, (0, 2, 1, 3)).astype(v_new.dtype)


def rg_lru_branch(xr, gr, conv_buf, h0, conv_w, conv_b, wa, ba, wx, bx, lru_lambda):
    B, S, W = xr.shape
    xc, new_buf = causal_dwconv(xr, conv_buf, conv_w, conv_b)
    xb = xc.reshape(B, S, LRU_BLOCKS, LRU_BW)
    r = jax.nn.sigmoid(jnp.einsum('bsnc,ncd->bsnd', xb, wa).reshape(B, S, W) + ba)
    i = jax.nn.sigmoid(jnp.einsum('bsnc,ncd->bsnd', xb, wx).reshape(B, S, W) + bx)
    log_a = (-LRU_C * r.astype(jnp.float32)) * jax.nn.softplus(-lru_lambda.astype(jnp.float32))
    a = jnp.exp(log_a)
    gx = jnp.sqrt(-jnp.expm1(2.0 * log_a)) * (i * xc).astype(jnp.float32)

    def step(h, inp):
        a_t, x_t = inp
        h = a_t * h + x_t
        return h, h

    h_last, hs = lax.scan(step, h0.astype(jnp.float32), (jnp.moveaxis(a, 1, 0), jnp.moveaxis(gx, 1, 0)))
    hs = jnp.moveaxis(hs, 0, 1).astype(xr.dtype)
    return hs * jax.nn.gelu(gr), new_buf, h_last.astype(h0.dtype)


def layer_block(x, attend, lru_buf, lru_h, ffn_buf, lam_init, P):
    B, S = x.shape[:2]
    h = rmsnorm(x, P['norm1_g'])
    z = h @ P['w_in']
    q, k, v, xr, gr, ga, gl = jnp.split(z, split_points(), axis=-1)
    q = rmsnorm(q.reshape(B, S, N_HEADS, 2, DK), P['q_norm_g']) * (DK ** -0.5)
    k = rmsnorm(k.reshape(B, S, N_HEADS, 2, DK), P['k_norm_g'])
    v = v.reshape(B, S, N_HEADS, DV)
    o = attend(q, k, v)
    o = rmsnorm(o, P['subln_g']) * (1.0 - lam_init)
    ya = o.reshape(B, S, ATTN_W) @ P['w_branch_attn']
    yl, lru_buf, lru_h = rg_lru_branch(xr, gr, lru_buf, lru_h, P['lru_conv_w'], P['lru_conv_b'],
                                       P['lru_wa'], P['lru_ba'], P['lru_wx'], P['lru_bx'], P['lru_lambda'])
    yr = yl @ P['w_branch_lru']
    mix = jax.nn.sigmoid(ga) * ya + jax.nn.sigmoid(gl) * yr
    x = x + mix @ P['w_out']
    h2 = rmsnorm(x, P['norm2_g'])
    up = h2 @ P['w_up']
    upc, ffn_buf = causal_dwconv(up, ffn_buf, P['ffn_conv_w'], P['ffn_conv_b'])
    u, g = jnp.split(upc, 2, axis=-1)
    x = x + (jax.nn.gelu(g) * u) @ P['w_down']
    return x, k.reshape(B, S, N_HEADS, 2 * DK), v, lru_h, lru_buf, ffn_buf


def setup_inputs(seed: int = 0) -> dict:
    key = jax.random.key(seed)
    ks = iter(jax.random.split(key, 48))
    f32 = jnp.float32

    def nrm(shape, scale):
        return jax.random.normal(next(ks), shape, f32) * scale

    def gain(shape):
        return 1.0 + nrm(shape, 0.01)

    n_pages = PAST_LEN // PAGE_SIZE
    n_pool = (DEC_BATCH * n_pages * 5) // 4
    x_prompt = nrm((BATCH, SEQ, D_MODEL), 1.0)
    x_sample = nrm((DEC_BATCH, DEC_SEQ, D_MODEL), 1.0)
    cache_k = nrm((DEPTH, n_pool, PAGE_SIZE, N_HEADS, 2 * DK), 1.0)
    cache_v = nrm((DEPTH, n_pool, PAGE_SIZE, N_HEADS, DV), 1.0)
    page_table = jax.random.permutation(next(ks), n_pool)[:DEC_BATCH * n_pages].reshape(DEC_BATCH, n_pages).astype(jnp.int32)
    state_lru_h = nrm((DEPTH, DEC_BATCH, LRU_W), 0.5)
    state_lru_conv = nrm((DEPTH, DEC_BATCH, LRU_CONV - 1, LRU_W), 1.0)
    state_ffn_conv = nrm((DEPTH, DEC_BATCH, FFN_CONV - 1, 2 * FF), 1.0)
    a0 = jax.random.uniform(next(ks), (DEPTH, LRU_W), f32, minval=0.9, maxval=0.999)
    return {
        'x_prompt': x_prompt, 'x_sample': x_sample,
        'cache_k': cache_k, 'cache_v': cache_v, 'page_table': page_table,
        'state_lru_h': state_lru_h, 'state_lru_conv': state_lru_conv, 'state_ffn_conv': state_ffn_conv,
        'norm1_g': gain((DEPTH, D_MODEL)),
        'w_in': nrm((DEPTH, D_MODEL, IN_W), D_MODEL ** -0.5),
        'q_norm_g': gain((DEPTH, DK)),
        'k_norm_g': gain((DEPTH, DK)),
        'lambda_q1': nrm((DEPTH, DK), 0.1),
        'lambda_k1': nrm((DEPTH, DK), 0.1),
        'lambda_q2': nrm((DEPTH, DK), 0.1),
        'lambda_k2': nrm((DEPTH, DK), 0.1),
        'subln_g': gain((DEPTH, DV)),
        'lru_conv_w': nrm((DEPTH, LRU_CONV, LRU_W), LRU_CONV ** -0.5),
        'lru_conv_b': nrm((DEPTH, LRU_W), 0.01),
        'lru_wa': nrm((DEPTH, LRU_BLOCKS, LRU_BW, LRU_BW), LRU_BW ** -0.5),
        'lru_ba': nrm((DEPTH, LRU_W), 0.01),
        'lru_wx': nrm((DEPTH, LRU_BLOCKS, LRU_BW, LRU_BW), LRU_BW ** -0.5),
        'lru_bx': nrm((DEPTH, LRU_W), 0.01),
        'lru_lambda': jnp.log(a0) - jnp.log1p(-a0),
        'w_branch_attn': nrm((DEPTH, ATTN_W, D_MODEL), ATTN_W ** -0.5),
        'w_branch_lru': nrm((DEPTH, LRU_W, D_MODEL), LRU_W ** -0.5),
        'w_out': nrm((DEPTH, D_MODEL, D_MODEL), D_MODEL ** -0.5),
        'norm2_g': gain((DEPTH, D_MODEL)),
        'w_up': nrm((DEPTH, D_MODEL, 2 * FF), D_MODEL ** -0.5),
        'ffn_conv_w': nrm((DEPTH, FFN_CONV, 2 * FF), FFN_CONV ** -0.5),
        'ffn_conv_b': nrm((DEPTH, 2 * FF), 0.01),
        'w_down': nrm((DEPTH, FF, D_MODEL), FF ** -0.5),
    }


def reference(x_prompt, x_sample, cache_k, cache_v, page_table, state_lru_h, state_lru_conv, state_ffn_conv,
              norm1_g, w_in, q_norm_g, k_norm_g, lambda_q1, lambda_k1, lambda_q2, lambda_k2, subln_g,
              lru_conv_w, lru_conv_b, lru_wa, lru_ba, lru_wx, lru_bx, lru_lambda,
              w_branch_attn, w_branch_lru, w_out, norm2_g, w_up, ffn_conv_w, ffn_conv_b, w_down):
    B, S = x_prompt.shape[:2]
    xp, xs = x_prompt, x_sample
    kp_l, vp_l, hp_l, cp_l, fp_l = [], [], [], [], []
    ks_l, vs_l, hs_l, cs_l, fs_l = [], [], [], [], []
    for l in range(DEPTH):
        P = dict(norm1_g=norm1_g[l], w_in=w_in[l], q_norm_g=q_norm_g[l], k_norm_g=k_norm_g[l],
                 subln_g=subln_g[l], lru_conv_w=lru_conv_w[l], lru_conv_b=lru_conv_b[l],
                 lru_wa=lru_wa[l], lru_ba=lru_ba[l], lru_wx=lru_wx[l], lru_bx=lru_bx[l],
                 lru_lambda=lru_lambda[l], w_branch_attn=w_branch_attn[l], w_branch_lru=w_branch_lru[l],
                 w_out=w_out[l], norm2_g=norm2_g[l], w_up=w_up[l], ffn_conv_w=ffn_conv_w[l],
                 ffn_conv_b=ffn_conv_b[l], w_down=w_down[l])
        lam_init = 0.8 - 0.6 * math.exp(-0.3 * l)
        lam = (jnp.exp(jnp.sum(lambda_q1[l].astype(jnp.float32) * lambda_k1[l].astype(jnp.float32)))
               - jnp.exp(jnp.sum(lambda_q2[l].astype(jnp.float32) * lambda_k2[l].astype(jnp.float32)))
               + lam_init)
        attend_prompt = lambda q, k, v: diff_attn_prompt(q, k, v, lam)
        attend_sample = lambda q, k, v: diff_attn_sample(q, k, v, cache_k, cache_v, page_table, l, lam)
        xp, kp, vp, hp, cp, fp = layer_block(
            xp, attend_prompt,
            jnp.zeros((B, LRU_CONV - 1, LRU_W), xp.dtype), jnp.zeros((B, LRU_W), xp.dtype),
            jnp.zeros((B, FFN_CONV - 1, 2 * FF), xp.dtype), lam_init, P)
        xs, ksn, vsn, hsn, csn, fsn = layer_block(
            xs, attend_sample, state_lru_conv[l], state_lru_h[l], state_ffn_conv[l], lam_init, P)
        kp_l.append(kp); vp_l.append(vp); hp_l.append(hp); cp_l.append(cp); fp_l.append(fp)
        ks_l.append(ksn); vs_l.append(vsn); hs_l.append(hsn); cs_l.append(csn); fs_l.append(fsn)
    y_prompt, y_sample = xp, xs
    k_prompt, v_prompt = jnp.stack(kp_l), jnp.stack(vp_l)
    lru_h_prompt, lru_conv_prompt, ffn_conv_prompt = jnp.stack(hp_l), jnp.stack(cp_l), jnp.stack(fp_l)
    k_sample, v_sample = jnp.stack(ks_l), jnp.stack(vs_l)
    lru_h_sample, lru_conv_sample, ffn_conv_sample = jnp.stack(hs_l), jnp.stack(cs_l), jnp.stack(fs_l)
    return (y_prompt, y_sample, k_prompt, v_prompt, lru_h_prompt, lru_conv_prompt, ffn_conv_prompt,
            k_sample, v_sample, lru_h_sample, lru_conv_sample, ffn_conv_sample)
```

```python
import functools
import math

import numpy as np
import jax
import jax.numpy as jnp
from jax import lax
from jax.experimental import pallas as pl
from jax.experimental.pallas import tpu as pltpu

F32 = jnp.float32
BF16 = jnp.bfloat16

D_MODEL = 1024
N_HEADS = 8
DK = D_MODEL // N_HEADS // 2
DV = 2 * DK
ATTN_W = N_HEADS * DV
LRU_BLOCKS = 8
LRU_W = ((4 * D_MODEL // 3 + 127) // 128) * 128
LRU_BW = LRU_W // LRU_BLOCKS
LRU_CONV = 4
LRU_C = 8.0
FF = 3 * D_MODEL
FFN_CONV = 3
PAGE_SIZE = 128
EPS = 1e-6
LAM_INIT = 0.8 - 0.6 * math.exp(-0.3 * 0)

LANES = 128
SUBLANES = 8
NEG = -1e30
VMEM_LIMIT = 56 * 1024 * 1024

QKV_W = N_HEADS * 2 * DK
OFF_Q, OFF_K, OFF_V = 0, QKV_W, 2 * QKV_W
OFF_XR = 3 * QKV_W
OFF_GR = OFF_XR + LRU_W
OFF_GA = OFF_GR + LRU_W
OFF_GL = OFF_GA + D_MODEL
IN_W = OFF_GL + D_MODEL


def _round_up(x, m):
    return (x + m - 1) // m * m


def _resident(shape):
    nd = len(shape)
    return pl.BlockSpec(shape, lambda *_: (0,) * nd, pipeline_mode=pl.Buffered(1))


def _rms(x, g):
    ms = jnp.mean(x * x, axis=-1, keepdims=True)
    return x * lax.rsqrt(ms + EPS) * g


def _lam(lamv_ref):
    lv = lamv_ref[...]
    s1 = jnp.sum(lv[0:1] * lv[1:2], axis=-1, keepdims=True)
    s2 = jnp.sum(lv[2:3] * lv[3:4], axis=-1, keepdims=True)
    return jnp.exp(s1) - jnp.exp(s2) + LAM_INIT


IN_PROJ_TM = 256
IN_PROJ_TN = 512


def _in_proj_kernel(x_ref, g1_ref, w_ref, qg_ref, kg_ref, gm_ref,
                    q_ref, kb_ref, k_ref, vb_ref, v_ref, xr_ref, gr_ref, ga_ref, gl_ref):
    h = _rms(x_ref[...], g1_ref[...]).astype(BF16)

    def proj(c0, c1):
        return jnp.dot(h, w_ref[:, c0:c1], preferred_element_type=F32)

    def group_norm(z, g):
        outs = []
        for j in range(z.shape[1] // LANES):
            zj = z[:, j * LANES:(j + 1) * LANES]
            ms = jnp.dot((zj * zj).astype(BF16), gm_ref[...], preferred_element_type=F32)
            outs.append(zj * lax.rsqrt(ms + EPS) * g)
        return jnp.concatenate(outs, axis=1)

    tn = IN_PROJ_TN
    for c in range(0, QKV_W, tn):
        q = group_norm(proj(OFF_Q + c, OFF_Q + c + tn), qg_ref[...])
        q_ref[:, c:c + tn] = q.astype(BF16)
        k = group_norm(proj(OFF_K + c, OFF_K + c + tn), kg_ref[...])
        k_ref[:, c:c + tn] = k
        kb_ref[:, c:c + tn] = k.astype(BF16)
        v = proj(OFF_V + c, OFF_V + c + tn)
        v_ref[:, c:c + tn] = v
        vb_ref[:, c:c + tn] = v.astype(BF16)
        ga_ref[:, c:c + tn] = proj(OFF_GA + c, OFF_GA + c + tn)
        gl_ref[:, c:c + tn] = proj(OFF_GL + c, OFF_GL + c + tn)
    for c in range(0, LRU_W, tn):
        c1 = min(c + tn, LRU_W)
        xr_ref[:, c:c1] = proj(OFF_XR + c, OFF_XR + c1)
        gr_ref[:, c:c1] = proj(OFF_GR + c, OFF_GR + c1)


def _in_proj(x2d, g1, w_in_bf, qg, kg, gmat):
    t = x2d.shape[0]
    tm = IN_PROJ_TM
    row = lambda w: pl.BlockSpec((tm, w), lambda i: (i, 0))
    out_shapes = [
        jax.ShapeDtypeStruct((t, QKV_W), BF16),
        jax.ShapeDtypeStruct((t, QKV_W), BF16),
        jax.ShapeDtypeStruct((t, QKV_W), F32),
        jax.ShapeDtypeStruct((t, QKV_W), BF16),
        jax.ShapeDtypeStruct((t, QKV_W), F32),
        jax.ShapeDtypeStruct((t, LRU_W), F32),
        jax.ShapeDtypeStruct((t, LRU_W), F32),
        jax.ShapeDtypeStruct((t, D_MODEL), F32),
        jax.ShapeDtypeStruct((t, D_MODEL), F32),
    ]
    return pl.pallas_call(
        _in_proj_kernel,
        grid=(t // tm,),
        in_specs=[row(D_MODEL), _resident((1, D_MODEL)), _resident((D_MODEL, IN_W)),
                  _resident((1, LANES)), _resident((1, LANES)), _resident((LANES, LANES))],
        out_specs=[row(s.shape[1]) for s in out_shapes],
        out_shape=out_shapes,
        compiler_params=pltpu.CompilerParams(dimension_semantics=("arbitrary",),
                                             vmem_limit_bytes=VMEM_LIMIT),
        name="in_proj",
    )(x2d, g1, w_in_bf, qg, kg, gmat)


ATT_BQ = 512


def _prompt_attn_kernel(qi_ref, ki_ref, q_ref, k_ref, v_ref, slope_ref, lamv_ref, sg_ref,
                        o_ref, qa_s, qb_s, m_s, l_s, acc_s):
    blk = ATT_BQ
    step = pl.program_id(2)
    qi = qi_ref[step]
    ki = ki_ref[step]

    @pl.when(ki == 0)
    def _():
        q = q_ref[0]
        lane = lax.broadcasted_iota(jnp.int32, q.shape, 1)
        qa_s[...] = jnp.where(lane < DK, q, jnp.zeros_like(q))
        qb_s[...] = jnp.where(lane >= DK, q, jnp.zeros_like(q))
        m_s[...] = jnp.full(m_s.shape, NEG, F32)
        l_s[...] = jnp.zeros(l_s.shape, F32)
        acc_s[...] = jnp.zeros(acc_s.shape, F32)

    def update(masked):
        k = k_ref[0]
        v = v_ref[0]
        slope = slope_ref[0][:, :1]
        kpos = ((ki - qi) * blk).astype(F32) + lax.broadcasted_iota(jnp.int32, (1, blk), 1).astype(F32)
        bias = slope * kpos
        if masked:
            row = lax.broadcasted_iota(jnp.int32, (blk, blk), 0)
            col = lax.broadcasted_iota(jnp.int32, (blk, blk), 1)
            keep = col <= row
        for idx, qs in enumerate((qa_s, qb_s)):
            s = lax.dot_general(qs[...], k, (((1,), (1,)), ((), ())), preferred_element_type=F32) + bias
            if masked:
                s = jnp.where(keep, s, NEG)
            m_old = m_s[idx]
            m_new = jnp.maximum(m_old, jnp.max(s, axis=-1, keepdims=True))
            p = jnp.exp(s - m_new)
            alpha = jnp.exp(m_old - m_new)
            l_s[idx] = alpha * l_s[idx] + jnp.sum(p, axis=-1, keepdims=True)
            acc_s[idx] = alpha * acc_s[idx] + jnp.dot(p.astype(BF16), v, preferred_element_type=F32)
            m_s[idx] = m_new

    @pl.when(ki < qi)
    def _():
        update(False)

    @pl.when(ki == qi)
    def _():
        update(True)
        lam = _lam(lamv_ref)
        a = acc_s[0] / l_s[0] - lam * (acc_s[1] / l_s[1])
        o_ref[0] = (_rms(a, sg_ref[...]) * (1.0 - LAM_INIT)).astype(BF16)


def _prompt_attn(q, k, v, slopes, lamv, sg):
    b, s, _ = q.shape
    blk = ATT_BQ
    nq = s // blk
    pairs = [(i, j) for i in range(nq) for j in range(i + 1)]
    qi_tab = jnp.asarray([p[0] for p in pairs], jnp.int32)
    ki_tab = jnp.asarray([p[1] for p in pairs], jnp.int32)
    grid_spec = pltpu.PrefetchScalarGridSpec(
        num_scalar_prefetch=2,
        grid=(b, N_HEADS, len(pairs)),
        in_specs=[
            pl.BlockSpec((1, blk, LANES), lambda bb, h, t, qt, kt: (bb, qt[t], h)),
            pl.BlockSpec((1, blk, LANES), lambda bb, h, t, qt, kt: (bb, kt[t], h)),
            pl.BlockSpec((1, blk, LANES), lambda bb, h, t, qt, kt: (bb, kt[t], h)),
            pl.BlockSpec((1, 1, LANES), lambda bb, h, t, qt, kt: (h, 0, 0)),
            pl.BlockSpec((4, DK), lambda bb, h, t, qt, kt: (0, 0)),
            pl.BlockSpec((1, DV), lambda bb, h, t, qt, kt: (0, 0)),
        ],
        out_specs=pl.BlockSpec((1, blk, LANES), lambda bb, h, t, qt, kt: (bb, qt[t], h)),
        scratch_shapes=[
            pltpu.VMEM((blk, LANES), BF16), pltpu.VMEM((blk, LANES), BF16),
            pltpu.VMEM((2, blk, 1), F32), pltpu.VMEM((2, blk, 1), F32),
            pltpu.VMEM((2, blk, DV), F32),
        ],
    )
    return pl.pallas_call(
        _prompt_attn_kernel,
        grid_spec=grid_spec,
        out_shape=jax.ShapeDtypeStruct((b, s, ATTN_W), BF16),
        compiler_params=pltpu.CompilerParams(
            dimension_semantics=("arbitrary", "arbitrary", "arbitrary"),
            vmem_limit_bytes=VMEM_LIMIT),
        name="prompt_attn",
    )(qi_tab, ki_tab, q, k, v, slopes, lamv, sg)


DEC_P = 8
PAIR_ROWS = 2 * PAGE_SIZE
QP_ROWS = 32


def _decode_kernel(pt_ref, q_ref, kn_ref, vn_ref, bias_ref, biasn_ref, slope_ref, lamv_ref, sg_ref,
                   *rest, n_chunks):
    del pt_ref
    kp = rest[:DEC_P]
    vp = rest[DEC_P:2 * DEC_P]
    o_ref = rest[2 * DEC_P]
    qs, m_s, l_s, acc_s = rest[2 * DEC_P + 1:]
    c = pl.program_id(1)
    t_new = q_ref.shape[1]
    nt = (((1,), (1,)), ((), ()))

    @pl.when(c == 0)
    def _():
        q = q_ref[0].astype(F32)
        lane = lax.broadcasted_iota(jnp.int32, (t_new, LANES), 1)
        for p in range(4):
            parts = []
            for hs in range(2):
                h = p + 4 * hs
                qh = q[:, h * LANES:(h + 1) * LANES]
                parts.append(jnp.where(lane < DK, qh, 0.0))
                parts.append(jnp.where(lane >= DK, qh, 0.0))
            qs[p] = jnp.concatenate(parts, axis=0).astype(BF16)
        m_s[...] = jnp.full(m_s.shape, NEG, F32)
        l_s[...] = jnp.zeros(l_s.shape, F32)
        acc_s[...] = jnp.zeros(acc_s.shape, F32)

    def online_update(p, u, shift_back, pv_fn):
        m_old = m_s[p]
        m_new = jnp.maximum(m_old, jnp.max(u, axis=-1, keepdims=True) + shift_back)
        pe = jnp.exp(u - (m_new - shift_back))
        alpha = jnp.exp(m_old - m_new)
        l_s[p] = alpha * l_s[p] + jnp.sum(pe, axis=-1, keepdims=True)
        acc_s[p] = alpha * acc_s[p] + pv_fn(pe.astype(BF16))
        m_s[p] = m_new

    chunk_off = ((c + 1 - n_chunks) * (DEC_P * PAGE_SIZE)).astype(F32)
    for p in range(4):
        qp = qs[p]
        scores = []
        for i in range(DEC_P):
            kk = kp[i][0, pl.ds(p, PAIR_ROWS, stride=4), :].astype(BF16)
            scores.append(lax.dot_general(qp, kk, nt, preferred_element_type=F32))
        u = jnp.concatenate(scores, axis=1) + bias_ref[p]

        def pv_fn(pb, p=p):
            acc = None
            for i in range(DEC_P):
                vv = vp[i][0, pl.ds(p, PAIR_ROWS, stride=4), :].astype(BF16)
                d = jnp.dot(pb[:, i * PAIR_ROWS:(i + 1) * PAIR_ROWS], vv, preferred_element_type=F32)
                acc = d if acc is None else acc + d
            return acc

        online_update(p, u, slope_ref[p] * chunk_off, pv_fn)

    @pl.when(c == n_chunks - 1)
    def _():
        lam = _lam(lamv_ref)
        for p in range(4):
            kn = kn_ref[0, pl.ds(p, 2 * t_new, stride=4), :].astype(BF16)
            vn = vn_ref[0, pl.ds(p, 2 * t_new, stride=4), :].astype(BF16)
            u = lax.dot_general(qs[p], kn, nt, preferred_element_type=F32) + biasn_ref[p]
            online_update(p, u, 0.0, lambda pb, vn=vn: jnp.dot(pb, vn, preferred_element_type=F32))
            out = acc_s[p] / l_s[p]
            for hs in range(2):
                h = p + 4 * hs
                r0 = hs * 2 * t_new
                a = out[r0:r0 + t_new] - lam * out[r0 + t_new:r0 + 2 * t_new]
                o_ref[0, :, h * LANES:(h + 1) * LANES] = _rms(a, sg_ref[...]) * (1.0 - LAM_INIT)


def _decode_tables(n_pages, t_new):
    slopes = 2.0 ** (-8.0 * np.arange(1, N_HEADS + 1) / N_HEADS)
    rows = np.arange(QP_ROWS)
    row_hs, row_t = rows // (2 * t_new), rows % t_new
    lane = np.arange(PAIR_ROWS * DEC_P)
    lane_hs = lane % 2
    lane_key = (lane % PAIR_ROWS) // 2 + (lane // PAIR_ROWS) * PAGE_SIZE
    lane_n = np.arange(2 * t_new)
    bias = np.zeros((4, QP_ROWS, lane.size), np.float32)
    biasn = np.zeros((4, QP_ROWS, lane_n.size), np.float32)
    slope_rows = np.zeros((4, QP_ROWS, 1), np.float32)
    for p in range(4):
        sl = slopes[p + 4 * row_hs]
        slope_rows[p, :, 0] = sl
        match = row_hs[:, None] == lane_hs[None, :]
        rel = lane_key[None, :] - DEC_P * PAGE_SIZE
        bias[p] = np.where(match, sl[:, None] * rel, NEG)
        match_n = (row_hs[:, None] == (lane_n % 2)[None, :]) & ((lane_n // 2)[None, :] <= row_t[:, None])
        biasn[p] = np.where(match_n, sl[:, None] * (lane_n // 2)[None, :], NEG)
    return jnp.asarray(bias), jnp.asarray(biasn), jnp.asarray(slope_rows)


def _decode_attn(page_table, q, kn, vn, cache_k, cache_v, lamv, sg):
    bd, t_new, _ = q.shape
    n_pages = page_table.shape[1]
    n_chunks = n_pages // DEC_P
    n_pool = cache_k.shape[0]
    bias, biasn, slope_rows = _decode_tables(n_pages, t_new)
    page_rows = PAGE_SIZE * N_HEADS

    def page_spec(i):
        return pl.BlockSpec((1, page_rows, LANES), lambda b, c, pt, i=i: (pt[b, c * DEC_P + i], 0, 0))

    const = lambda shape: pl.BlockSpec(shape, lambda b, c, pt: (0,) * len(shape))
    per_b = lambda shape: pl.BlockSpec(shape, lambda b, c, pt: (b,) + (0,) * (len(shape) - 1))
    grid_spec = pltpu.PrefetchScalarGridSpec(
        num_scalar_prefetch=1,
        grid=(bd, n_chunks),
        in_specs=[per_b((1, t_new, ATTN_W)), per_b((1, t_new * N_HEADS, LANES)),
                  per_b((1, t_new * N_HEADS, LANES)),
                  const(bias.shape), const(biasn.shape), const(slope_rows.shape),
                  const((4, DK)), const((1, DV))]
                 + [page_spec(i) for i in range(DEC_P)] + [page_spec(i) for i in range(DEC_P)],
        out_specs=per_b((1, t_new, ATTN_W)),
        scratch_shapes=[pltpu.VMEM((4, QP_ROWS, LANES), BF16), pltpu.VMEM((4, QP_ROWS, 1), F32),
                        pltpu.VMEM((4, QP_ROWS, 1), F32), pltpu.VMEM((4, QP_ROWS, DV), F32)],
    )
    ck = cache_k.reshape(n_pool, page_rows, LANES)
    cv = cache_v.reshape(n_pool, page_rows, LANES)
    return pl.pallas_call(
        functools.partial(_decode_kernel, n_chunks=n_chunks),
        grid_spec=grid_spec,
        out_shape=jax.ShapeDtypeStruct((bd, t_new, ATTN_W), F32),
        compiler_params=pltpu.CompilerParams(dimension_semantics=("arbitrary", "arbitrary"),
                                             vmem_limit_bytes=VMEM_LIMIT),
        name="decode_attn",
    )(page_table, q, kn, vn, bias, biasn, slope_rows, lamv, sg,
      *([ck] * DEC_P), *([cv] * DEC_P))


def _band(c0, c1):
    lo = (c0 // LRU_BW) * LRU_BW
    hi = ((c1 - 1) // LRU_BW + 1) * LRU_BW
    return lo // LANES * LANES, min(_round_up(hi, LANES), LRU_W)


def _lru_kernel(xr_ref, gr_ref, cs_ref, h0_ref, cw_ref, cb_ref, wa_ref, ba_ref, wx_ref, bx_ref, lam_ref,
                yl_ref, hl_ref, xp_s, h_s, a_s, gx_s, hs_s, *, rps, pad):
    tr = xr_ref.shape[1]
    hist = (LRU_CONV - 1) * rps

    @pl.when(pl.program_id(1) == 0)
    def _():
        xp_s[pad - hist:pad, :] = cs_ref[0]
        h_s[...] = h0_ref[0]

    xp_s[pad:pad + tr, :] = xr_ref[0]
    xc = cb_ref[...]
    for j in range(LRU_CONV):
        off = pad - (LRU_CONV - 1 - j) * rps
        xc = xc + cw_ref[j:j + 1, :] * xp_s[off:off + tr, :]
    xcb = xc.astype(BF16)

    def gate(w_ref, b_ref):
        outs = []
        for c0 in range(0, LRU_W, 2 * LANES):
            c1 = min(c0 + 2 * LANES, LRU_W)
            lo, hi = _band(c0, c1)
            outs.append(jnp.dot(xcb[:, lo:hi], w_ref[lo:hi, c0:c1], preferred_element_type=F32))
        return jax.nn.sigmoid(jnp.concatenate(outs, axis=1) + b_ref[...])

    r = gate(wa_ref, ba_ref)
    i = gate(wx_ref, bx_ref)
    z = -lam_ref[...]
    softplus = jnp.maximum(z, 0.0) + jnp.log1p(jnp.exp(-jnp.abs(z)))
    log_a = (-LRU_C * r) * softplus
    a = jnp.exp(log_a)
    a_s[...] = a
    gx_s[...] = jnp.sqrt(-jnp.tanh(log_a) * (a * a + 1.0)) * (i * xc)

    def step(t, h):
        rows = pl.ds(pl.multiple_of(t * rps, rps), rps)
        h = a_s[rows, :] * h + gx_s[rows, :]
        hs_s[rows, :] = h
        return h

    h = lax.fori_loop(0, tr // rps, step, h_s[...], unroll=8)
    h_s[...] = h
    hl_ref[0] = h
    yl_ref[0] = (hs_s[...] * jax.nn.gelu(gr_ref[0])).astype(BF16)
    xp_s[pad - hist:pad, :] = xp_s[pad + tr - hist:pad + tr, :]


def _lru(xr, gr, conv_state, h0, cw, cb, wa, ba, wx, bx, lam, *, rps, tr):
    nb, rows, _ = xr.shape
    pad = _round_up((LRU_CONV - 1) * rps, SUBLANES)
    tile = pl.BlockSpec((1, tr, LRU_W), lambda b, s: (b, s, 0))
    per_b = lambda shape: pl.BlockSpec(shape, lambda b, s: (b, 0, 0))
    return pl.pallas_call(
        functools.partial(_lru_kernel, rps=rps, pad=pad),
        grid=(nb, rows // tr),
        in_specs=[tile, tile, per_b((1, (LRU_CONV - 1) * rps, LRU_W)), per_b((1, rps, LRU_W)),
                  _resident((LRU_CONV, LRU_W)), _resident((1, LRU_W)),
                  _resident((LRU_W, LRU_W)), _resident((1, LRU_W)),
                  _resident((LRU_W, LRU_W)), _resident((1, LRU_W)), _resident((1, LRU_W))],
        out_specs=[tile, per_b((1, rps, LRU_W))],
        out_shape=[jax.ShapeDtypeStruct((nb, rows, LRU_W), BF16),
                   jax.ShapeDtypeStruct((nb, rps, LRU_W), F32)],
        scratch_shapes=[pltpu.VMEM((pad + tr, LRU_W), F32), pltpu.VMEM((rps, LRU_W), F32),
                        pltpu.VMEM((tr, LRU_W), F32), pltpu.VMEM((tr, LRU_W), F32),
                        pltpu.VMEM((tr, LRU_W), F32)],
        compiler_params=pltpu.CompilerParams(dimension_semantics=("arbitrary", "arbitrary"),
                                             vmem_limit_bytes=VMEM_LIMIT),
        name="rg_lru",
    )(xr, gr, conv_state, h0, cw, cb, wa, ba, wx, bx, lam)


FFN_TC = 512


def _post_kernel(x_ref, o_ref, yl_ref, ga_ref, gl_ref, fs_ref,
                 wba_ref, wbl_ref, wo_ref, g2_ref, wup_ref, fw_ref, fb_ref, wdn_ref,
                 y_ref, fso_ref, tail_s, xp_s, *, rps, pad):
    tr = x_ref.shape[1]
    hist = (FFN_CONV - 1) * rps

    @pl.when(pl.program_id(1) == 0)
    def _():
        tail_s[pad - hist:pad, :] = fs_ref[0]

    ya = jnp.dot(o_ref[0], wba_ref[...], preferred_element_type=F32)
    yr = jnp.dot(yl_ref[0], wbl_ref[...], preferred_element_type=F32)
    mix = jax.nn.sigmoid(ga_ref[0]) * ya + jax.nn.sigmoid(gl_ref[0]) * yr
    x1 = x_ref[0] + jnp.dot(mix.astype(BF16), wo_ref[...], preferred_element_type=F32)
    h2 = _rms(x1, g2_ref[...]).astype(BF16)

    def conv_cols(off):
        up = jnp.dot(h2, wup_ref[:, off:off + FFN_TC], preferred_element_type=F32)
        xp_s[pad - hist:pad, :] = tail_s[pad - hist:pad, off:off + FFN_TC]
        xp_s[pad:pad + tr, :] = up
        tail_s[pad - hist:pad, off:off + FFN_TC] = up[tr - hist:tr, :]
        out = fb_ref[:, off:off + FFN_TC]
        for j in range(FFN_CONV):
            r0 = pad - (FFN_CONV - 1 - j) * rps
            out = out + fw_ref[j:j + 1, off:off + FFN_TC] * xp_s[r0:r0 + tr, :]
        return out

    acc = x1
    for c in range(0, FF, FFN_TC):
        u = conv_cols(c)
        g = conv_cols(FF + c)
        act = (jax.nn.gelu(g) * u).astype(BF16)
        acc = acc + jnp.dot(act, wdn_ref[c:c + FFN_TC, :], preferred_element_type=F32)
    y_ref[0] = acc
    fso_ref[0] = tail_s[pad - hist:pad, :]


def _post(x, o, yl, ga, gl, ffn_state, wba, wbl, wo, g2, wup, fw, fb, wdn, *, rps, tr):
    nb, rows, _ = x.shape
    pad = _round_up((FFN_CONV - 1) * rps, SUBLANES)
    tile = lambda w: pl.BlockSpec((1, tr, w), lambda b, s: (b, s, 0))
    state = pl.BlockSpec((1, (FFN_CONV - 1) * rps, 2 * FF), lambda b, s: (b, 0, 0))
    return pl.pallas_call(
        functools.partial(_post_kernel, rps=rps, pad=pad),
        grid=(nb, rows // tr),
        in_specs=[tile(D_MODEL), tile(ATTN_W), tile(LRU_W), tile(D_MODEL), tile(D_MODEL), state,
                  _resident((ATTN_W, D_MODEL)), _resident((LRU_W, D_MODEL)), _resident((D_MODEL, D_MODEL)),
                  _resident((1, D_MODEL)), _resident((D_MODEL, 2 * FF)), _resident((FFN_CONV, 2 * FF)),
                  _resident((1, 2 * FF)), _resident((FF, D_MODEL))],
        out_specs=[tile(D_MODEL), state],
        out_shape=[jax.ShapeDtypeStruct((nb, rows, D_MODEL), F32),
                   jax.ShapeDtypeStruct((nb, (FFN_CONV - 1) * rps, 2 * FF), F32)],
        scratch_shapes=[pltpu.VMEM((pad, 2 * FF), F32), pltpu.VMEM((pad + tr, FFN_TC), F32)],
        compiler_params=pltpu.CompilerParams(dimension_semantics=("arbitrary", "arbitrary"),
                                             vmem_limit_bytes=VMEM_LIMIT),
        name="out_ffn",
    )(x, o, yl, ga, gl, ffn_state, wba, wbl, wo, g2, wup, fw, fb, wdn)


def _block_diag(w):
    n, bw, _ = w.shape
    eye = jnp.eye(n, dtype=w.dtype)
    return jnp.einsum('nm,ncd->ncmd', eye, w).reshape(n * bw, n * bw)


SAMPLE_GROUPS = 4


def _time_major(a):
    b, t, c = a.shape
    g = SAMPLE_GROUPS
    return jnp.swapaxes(a.reshape(g, b // g, t, c), 1, 2).reshape(g, t * (b // g), c)


def _batch_major(a, t):
    g, rows, c = a.shape
    return jnp.swapaxes(a.reshape(g, t, rows // t, c), 1, 2).reshape(g * (rows // t), t, c)


def kernel(x_prompt, x_sample, cache_k, cache_v, page_table, state_lru_h, state_lru_conv, state_ffn_conv, norm1_g, w_in, q_norm_g, k_norm_g, lambda_q1, lambda_k1, lambda_q2, lambda_k2, subln_g, lru_conv_w, lru_conv_b, lru_wa, lru_ba, lru_wx, lru_bx, lru_lambda, w_branch_attn, w_branch_lru, w_out, norm2_g, w_up, ffn_conv_w, ffn_conv_b, w_down):
    assert w_in.shape[0] == 1, "single-layer step"
    b, s, _ = x_prompt.shape
    bd, t_new, _ = x_sample.shape
    row = lambda a: a.reshape(1, -1)

    w_in_bf = w_in[0].astype(BF16)
    g1 = row(norm1_g[0])
    qg = row(jnp.tile(q_norm_g[0], 2)) * (DK ** -0.5)
    kg = row(jnp.tile(k_norm_g[0], 2))
    gid = np.arange(LANES) // DK
    gmat = jnp.asarray((gid[:, None] == gid[None, :]).astype(np.float32) / DK, BF16)
    lamv = jnp.stack([lambda_q1[0], lambda_k1[0], lambda_q2[0], lambda_k2[0]]).astype(F32)
    sg = row(subln_g[0])
    slopes = jnp.asarray(np.broadcast_to(
        (2.0 ** (-8.0 * np.arange(1, N_HEADS + 1) / N_HEADS)).astype(np.float32)[:, None, None],
        (N_HEADS, 1, LANES)))
    wa = _block_diag(lru_wa[0]).astype(BF16)
    wx = _block_diag(lru_wx[0]).astype(BF16)
    lru_params = (lru_conv_w[0], row(lru_conv_b[0]), wa, row(lru_ba[0]), wx, row(lru_bx[0]),
                  row(lru_lambda[0]))
    post_params = (w_branch_attn[0].astype(BF16), w_branch_lru[0].astype(BF16), w_out[0].astype(BF16),
                   row(norm2_g[0]), w_up[0].astype(BF16), ffn_conv_w[0], row(ffn_conv_b[0]),
                   w_down[0].astype(BF16))

    qp, kbp, kp, vbp, vp, xrp, grp, gap, glp = _in_proj(
        x_prompt.reshape(b * s, D_MODEL), g1, w_in_bf, qg, kg, gmat)
    r3 = lambda a: a.reshape(b, s, a.shape[-1])
    o_p = _prompt_attn(r3(qp), r3(kbp), r3(vbp), slopes, lamv, sg)
    xrp3 = r3(xrp)
    yl_p, h_p = _lru(xrp3, r3(grp), jnp.zeros((b, LRU_CONV - 1, LRU_W), F32), jnp.zeros((b, 1, LRU_W), F32),
                     *lru_params, rps=1, tr=256)
    y_p, fs_p = _post(x_prompt, o_p, yl_p, r3(gap), r3(glp), jnp.zeros((b, FFN_CONV - 1, 2 * FF), F32),
                      *post_params, rps=1, tr=256)

    qs_, kbs, ks, vbs, vs, xrs, grs, gas, gls = _in_proj(
        x_sample.reshape(bd * t_new, D_MODEL), g1, w_in_bf, qg, kg, gmat)
    del kbs, vbs
    o_s = _decode_attn(page_table, qs_.astype(F32).reshape(bd, t_new, ATTN_W),
                       ks.reshape(bd, t_new * N_HEADS, LANES), vs.reshape(bd, t_new * N_HEADS, LANES),
                       cache_k[0], cache_v[0], lamv, sg)
    d3 = lambda a: a.reshape(bd, t_new, a.shape[-1])
    bg = bd // SAMPLE_GROUPS
    xrs_t = _time_major(d3(xrs))
    yl_s, h_s = _lru(xrs_t, _time_major(d3(grs)), _time_major(state_lru_conv[0]),
                     state_lru_h[0].reshape(SAMPLE_GROUPS, bg, LRU_W), *lru_params, rps=bg, tr=bg * t_new)
    y_s, fs_s = _post(_time_major(x_sample), _time_major(o_s.astype(BF16)), yl_s,
                      _time_major(d3(gas)), _time_major(d3(gls)),
                      _time_major(state_ffn_conv[0]), *post_params, rps=bg, tr=bg * t_new)

    hd = lambda a, n, t: a.reshape(1, n, t, N_HEADS, LANES)
    return (y_p, _batch_major(y_s, t_new),
            hd(kp, b, s), hd(vp, b, s),
            h_p.reshape(1, b, LRU_W), xrp3[None, :, s - (LRU_CONV - 1):, :], fs_p[None],
            hd(ks, bd, t_new), hd(vs, bd, t_new),
            h_s.reshape(1, bd, LRU_W),
            _batch_major(xrs_t[:, (t_new - LRU_CONV + 1) * bg:, :], LRU_CONV - 1)[None],
            _batch_major(fs_s, FFN_CONV - 1)[None])
```

```python
import functools
import math

import numpy as np
import jax
import jax.numpy as jnp
from jax import lax
from jax.experimental import pallas as pl
from jax.experimental.pallas import tpu as pltpu

F32 = jnp.float32
BF16 = jnp.bfloat16

D_MODEL = 1024
N_HEADS = 8
DK = D_MODEL // N_HEADS // 2
DV = 2 * DK
ATTN_W = N_HEADS * DV
LRU_BLOCKS = 8
LRU_W = ((4 * D_MODEL // 3 + 127) // 128) * 128
LRU_BW = LRU_W // LRU_BLOCKS
LRU_CONV = 4
LRU_C = 8.0
FF = 3 * D_MODEL
FFN_CONV = 3
PAGE_SIZE = 128
EPS = 1e-6
LAM_INIT = 0.8 - 0.6 * math.exp(-0.3 * 0)
LOG2E = math.log2(math.e)
GELU_C = math.sqrt(2.0 / math.pi)

LANES = 128
SUBLANES = 8
NEG = -1e30
VMEM_LIMIT = 56 * 1024 * 1024

QKV_W = N_HEADS * 2 * DK
OFF_Q, OFF_K, OFF_V = 0, QKV_W, 2 * QKV_W
OFF_XR = 3 * QKV_W
OFF_GR = OFF_XR + LRU_W
OFF_GA = OFF_GR + LRU_W
OFF_GL = OFF_GA + D_MODEL
IN_W = OFF_GL + D_MODEL

NT_DIMS = (((1,), (1,)), ((), ()))


def _round_up(x, m):
    return (x + m - 1) // m * m


def _resident(shape):
    nd = len(shape)
    return pl.BlockSpec(shape, lambda *_: (0,) * nd, pipeline_mode=pl.Buffered(1))


def _rms(x, g):
    ms = jnp.mean(x * x, axis=-1, keepdims=True)
    return x * lax.rsqrt(ms + EPS) * g


def _sigmoid(x):
    return 0.5 * jnp.tanh(0.5 * x) + 0.5


def _gelu(x):
    hx = 0.5 * x
    return hx + hx * jnp.tanh(x * (GELU_C + (GELU_C * 0.044715) * (x * x)))


def _lam(lamv_ref):
    lv = lamv_ref[...]
    s1 = jnp.sum(lv[0:1] * lv[1:2], axis=-1, keepdims=True)
    s2 = jnp.sum(lv[2:3] * lv[3:4], axis=-1, keepdims=True)
    return jnp.exp(s1) - jnp.exp(s2) + LAM_INIT


IN_PROJ_TM = 256
IN_PROJ_TN = 512


def _in_proj_kernel(x_ref, g1_ref, w_ref, qg_ref, kg_ref, gm_ref,
                    q_ref, kb_ref, k_ref, vb_ref, v_ref, xr_ref, gr_ref, ga_ref, gl_ref):
    h = _rms(x_ref[...], g1_ref[...]).astype(BF16)

    def proj(c0, c1):
        return jnp.dot(h, w_ref[:, c0:c1], preferred_element_type=F32)

    def group_norm(z, g):
        outs = []
        for j in range(z.shape[1] // LANES):
            zj = z[:, j * LANES:(j + 1) * LANES]
            ms = jnp.dot((zj * zj).astype(BF16), gm_ref[...], preferred_element_type=F32)
            outs.append(zj * lax.rsqrt(ms + EPS) * g)
        return jnp.concatenate(outs, axis=1)

    tn = IN_PROJ_TN
    for c in range(0, QKV_W, tn):
        q = group_norm(proj(OFF_Q + c, OFF_Q + c + tn), qg_ref[...])
        q_ref[:, c:c + tn] = q.astype(BF16)
        k = group_norm(proj(OFF_K + c, OFF_K + c + tn), kg_ref[...])
        k_ref[:, c:c + tn] = k
        kb_ref[:, c:c + tn] = k.astype(BF16)
        v = proj(OFF_V + c, OFF_V + c + tn)
        v_ref[:, c:c + tn] = v
        vb_ref[:, c:c + tn] = v.astype(BF16)
        ga_ref[:, c:c + tn] = proj(OFF_GA + c, OFF_GA + c + tn)
        gl_ref[:, c:c + tn] = proj(OFF_GL + c, OFF_GL + c + tn)
    for c in range(0, LRU_W, tn):
        c1 = min(c + tn, LRU_W)
        xr_ref[:, c:c1] = proj(OFF_XR + c, OFF_XR + c1)
        gr_ref[:, c:c1] = proj(OFF_GR + c, OFF_GR + c1)


def _in_proj(x2d, g1, w_in_bf, qg, kg, gmat):
    t = x2d.shape[0]
    tm = IN_PROJ_TM
    row = lambda w: pl.BlockSpec((tm, w), lambda i: (i, 0))
    out_shapes = [
        jax.ShapeDtypeStruct((t, QKV_W), BF16),
        jax.ShapeDtypeStruct((t, QKV_W), BF16),
        jax.ShapeDtypeStruct((t, QKV_W), F32),
        jax.ShapeDtypeStruct((t, QKV_W), BF16),
        jax.ShapeDtypeStruct((t, QKV_W), F32),
        jax.ShapeDtypeStruct((t, LRU_W), F32),
        jax.ShapeDtypeStruct((t, LRU_W), F32),
        jax.ShapeDtypeStruct((t, D_MODEL), F32),
        jax.ShapeDtypeStruct((t, D_MODEL), F32),
    ]
    return pl.pallas_call(
        _in_proj_kernel,
        grid=(t // tm,),
        in_specs=[row(D_MODEL), _resident((1, D_MODEL)), _resident((D_MODEL, IN_W)),
                  _resident((1, LANES)), _resident((1, LANES)), _resident((LANES, LANES))],
        out_specs=[row(s.shape[1]) for s in out_shapes],
        out_shape=out_shapes,
        compiler_params=pltpu.CompilerParams(dimension_semantics=("arbitrary",),
                                             vmem_limit_bytes=VMEM_LIMIT),
        name="in_proj",
    )(x2d, g1, w_in_bf, qg, kg, gmat)


ATT_BLK = 512


def _prompt_attn_kernel(q_ref, k_ref, v_ref, slope_ref, lamv_ref, sg_ref, o_ref,
                        q2_s, vx_s, m_s, acc_s):
    blk = ATT_BLK
    qi = pl.program_id(2)

    @pl.when(qi == 0)
    def _():
        vx_s[:, :DV] = v_ref[0]
        vx_s[:, DV:] = jnp.ones((vx_s.shape[0], LANES), BF16)

    q = q_ref[0]
    lane = lax.broadcasted_iota(jnp.int32, q.shape, 1)
    q2_s[:blk, :] = jnp.where(lane < DK, q, jnp.zeros_like(q))
    q2_s[blk:, :] = jnp.where(lane >= DK, q, jnp.zeros_like(q))
    m_s[...] = jnp.full(m_s.shape, NEG, F32)
    acc_s[...] = jnp.zeros(acc_s.shape, F32)

    slope = slope_ref[0][:, :1]
    col_f = lax.broadcasted_iota(jnp.int32, (1, blk), 1).astype(F32)

    def block(ki, masked):
        start = pl.multiple_of(ki * blk, blk)
        k = k_ref[0, pl.ds(start, blk), :]
        vx = vx_s[pl.ds(start, blk), :]
        bias = slope * (col_f + ((ki - qi) * blk).astype(F32))
        if masked:
            row = lax.broadcasted_iota(jnp.int32, (blk, blk), 0)
            col = lax.broadcasted_iota(jnp.int32, (blk, blk), 1)
            keep = col <= row
        for rows in (slice(0, blk), slice(blk, 2 * blk)):
            s = lax.dot_general(q2_s[rows, :], k, NT_DIMS, preferred_element_type=F32) + bias
            if masked:
                s = jnp.where(keep, s, NEG)
            m_prev = m_s[rows, :]
            m_next = jnp.maximum(m_prev, jnp.max(s, axis=-1, keepdims=True))
            p = jnp.exp2(s - jnp.tile(m_next, (1, blk // LANES)))
            alpha = jnp.exp2(m_prev - m_next)
            pv = jnp.dot(p.astype(BF16), vx, preferred_element_type=F32)
            acc_s[rows, :] = jnp.tile(alpha, (1, 2)) * acc_s[rows, :] + pv
            m_s[rows, :] = m_next

    def body(ki, carry):
        block(ki, False)
        return carry

    lax.fori_loop(0, qi, body, 0)
    block(qi, True)

    lam = _lam(lamv_ref)
    acc = acc_s[...]
    o1 = acc[:blk, :DV] / acc[:blk, DV:]
    o2 = acc[blk:, :DV] / acc[blk:, DV:]
    o_ref[0] = (_rms(o1 - lam * o2, sg_ref[...]) * (1.0 - LAM_INIT)).astype(BF16)


def _prompt_attn(q, k, v, slopes, lamv, sg):
    b, s, _ = q.shape
    blk = ATT_BLK
    seq = pl.BlockSpec((1, s, LANES), lambda bb, h, i: (bb, 0, h))
    tile = pl.BlockSpec((1, blk, LANES), lambda bb, h, i: (bb, i, h))
    return pl.pallas_call(
        _prompt_attn_kernel,
        grid=(b, N_HEADS, s // blk),
        in_specs=[tile, seq, seq,
                  pl.BlockSpec((1, 1, LANES), lambda bb, h, i: (h, 0, 0)),
                  pl.BlockSpec((4, DK), lambda bb, h, i: (0, 0)),
                  pl.BlockSpec((1, DV), lambda bb, h, i: (0, 0))],
        out_specs=tile,
        out_shape=jax.ShapeDtypeStruct((b, s, ATTN_W), BF16),
        scratch_shapes=[pltpu.VMEM((2 * blk, LANES), BF16), pltpu.VMEM((s, 2 * LANES), BF16),
                        pltpu.VMEM((2 * blk, LANES), F32), pltpu.VMEM((2 * blk, 2 * LANES), F32)],
        compiler_params=pltpu.CompilerParams(
            dimension_semantics=("arbitrary", "arbitrary", "arbitrary"),
            vmem_limit_bytes=VMEM_LIMIT),
        name="prompt_attn",
    )(q, k, v, slopes, lamv, sg)


DEC_P = 8
PAIR_ROWS = 2 * PAGE_SIZE
QP_ROWS = 32
N_PAIRS = N_HEADS // 2


def _decode_kernel(pt_ref, q_ref, kn_ref, vn_ref, bias_ref, biasn_ref, slope_ref, lamv_ref, sg_ref,
                   *rest, n_chunks):
    del pt_ref
    kp = rest[:DEC_P]
    vp = rest[DEC_P:2 * DEC_P]
    o_ref = rest[2 * DEC_P]
    qs, m_s, l_s, acc_s = rest[2 * DEC_P + 1:]
    c = pl.program_id(1)
    t_new = q_ref.shape[1]

    @pl.when(c == 0)
    def _():
        q = q_ref[0]
        lane = lax.broadcasted_iota(jnp.int32, (t_new, LANES), 1)
        for p in range(N_PAIRS):
            parts = []
            for hs in range(2):
                h = p + N_PAIRS * hs
                qh = q[:, h * LANES:(h + 1) * LANES]
                parts.append(jnp.where(lane < DK, qh, 0.0))
                parts.append(jnp.where(lane >= DK, qh, 0.0))
            qs[p] = jnp.concatenate(parts, axis=0).astype(BF16)
        m_s[...] = jnp.full(m_s.shape, NEG, F32)
        l_s[...] = jnp.zeros(l_s.shape, F32)
        acc_s[...] = jnp.zeros(acc_s.shape, F32)

    def pair_rows(refs, p, n):
        return jnp.concatenate([r[0, pl.ds(p, n, stride=N_PAIRS), :] for r in refs], axis=0).astype(BF16)

    def softmax_update(p, u, shift_back):
        m_old = m_s[p]
        m_new = jnp.maximum(m_old, jnp.max(u, axis=-1, keepdims=True) + shift_back)
        ref = m_new - shift_back
        n = u.shape[1]
        pe = jnp.exp2(u - (jnp.tile(ref, (1, n // LANES)) if n >= LANES else ref[:, :n]))
        alpha = jnp.exp2(m_old - m_new)
        l_s[p] = alpha * l_s[p] + jnp.sum(pe, axis=-1, keepdims=True)
        m_s[p] = m_new
        return pe.astype(BF16), alpha

    chunk_off = ((c + 1 - n_chunks) * (DEC_P * PAGE_SIZE)).astype(F32)
    scores = [lax.dot_general(qs[p], pair_rows(kp, p, PAIR_ROWS), NT_DIMS, preferred_element_type=F32)
              + bias_ref[p] for p in range(N_PAIRS)]
    probs = [softmax_update(p, scores[p], slope_ref[p] * chunk_off) for p in range(N_PAIRS)]
    for p in range(N_PAIRS):
        pb, alpha = probs[p]
        acc_s[p] = alpha * acc_s[p] + jnp.dot(pb, pair_rows(vp, p, PAIR_ROWS), preferred_element_type=F32)

    @pl.when(c == n_chunks - 1)
    def _():
        lam = _lam(lamv_ref)
        for p in range(N_PAIRS):
            kn = pair_rows([kn_ref], p, 2 * t_new)
            vn = pair_rows([vn_ref], p, 2 * t_new)
            u = lax.dot_general(qs[p], kn, NT_DIMS, preferred_element_type=F32) + biasn_ref[p]
            pb, alpha = softmax_update(p, u, 0.0)
            acc = alpha * acc_s[p] + jnp.dot(pb, vn, preferred_element_type=F32)
            out = acc / l_s[p]
            for hs in range(2):
                h = p + N_PAIRS * hs
                r0 = hs * 2 * t_new
                a = out[r0:r0 + t_new] - lam * out[r0 + t_new:r0 + 2 * t_new]
                o_ref[0, :, h * LANES:(h + 1) * LANES] = _rms(a, sg_ref[...]) * (1.0 - LAM_INIT)


def _decode_tables(t_new):
    slopes = LOG2E * 2.0 ** (-8.0 * np.arange(1, N_HEADS + 1) / N_HEADS)
    rows = np.arange(QP_ROWS)
    row_hs, row_t = rows // (2 * t_new), rows % t_new
    lane = np.arange(PAIR_ROWS * DEC_P)
    lane_hs = lane % 2
    lane_key = (lane % PAIR_ROWS) // 2 + (lane // PAIR_ROWS) * PAGE_SIZE
    lane_n = np.arange(2 * t_new)
    bias = np.zeros((N_PAIRS, QP_ROWS, lane.size), np.float32)
    biasn = np.zeros((N_PAIRS, QP_ROWS, lane_n.size), np.float32)
    slope_rows = np.zeros((N_PAIRS, QP_ROWS, LANES), np.float32)
    for p in range(N_PAIRS):
        sl = slopes[p + N_PAIRS * row_hs]
        slope_rows[p] = sl[:, None]
        match = row_hs[:, None] == lane_hs[None, :]
        rel = lane_key[None, :] - DEC_P * PAGE_SIZE
        bias[p] = np.where(match, sl[:, None] * rel, NEG)
        match_n = (row_hs[:, None] == (lane_n % 2)[None, :]) & ((lane_n // 2)[None, :] <= row_t[:, None])
        biasn[p] = np.where(match_n, sl[:, None] * (lane_n // 2)[None, :], NEG)
    return jnp.asarray(bias), jnp.asarray(biasn), jnp.asarray(slope_rows)


def _decode_attn(page_table, q, kn, vn, cache_k, cache_v, lamv, sg):
    bd, t_new, _ = q.shape
    n_pages = page_table.shape[1]
    n_chunks = n_pages // DEC_P
    n_pool = cache_k.shape[0]
    bias, biasn, slope_rows = _decode_tables(t_new)
    page_rows = PAGE_SIZE * N_HEADS

    def page_spec(i):
        return pl.BlockSpec((1, page_rows, LANES), lambda b, c, pt, i=i: (pt[b, c * DEC_P + i], 0, 0))

    const = lambda shape: pl.BlockSpec(shape, lambda b, c, pt: (0,) * len(shape))
    per_b = lambda shape: pl.BlockSpec(shape, lambda b, c, pt: (b,) + (0,) * (len(shape) - 1))
    grid_spec = pltpu.PrefetchScalarGridSpec(
        num_scalar_prefetch=1,
        grid=(bd, n_chunks),
        in_specs=[per_b((1, t_new, ATTN_W)), per_b((1, t_new * N_HEADS, LANES)),
                  per_b((1, t_new * N_HEADS, LANES)),
                  const(bias.shape), const(biasn.shape), const(slope_rows.shape),
                  const((4, DK)), const((1, DV))]
                 + [page_spec(i) for i in range(DEC_P)] + [page_spec(i) for i in range(DEC_P)],
        out_specs=per_b((1, t_new, ATTN_W)),
        scratch_shapes=[pltpu.VMEM((N_PAIRS, QP_ROWS, LANES), BF16), pltpu.VMEM((N_PAIRS, QP_ROWS, LANES), F32),
                        pltpu.VMEM((N_PAIRS, QP_ROWS, LANES), F32), pltpu.VMEM((N_PAIRS, QP_ROWS, DV), F32)],
    )
    ck = cache_k.reshape(n_pool, page_rows, LANES)
    cv = cache_v.reshape(n_pool, page_rows, LANES)
    return pl.pallas_call(
        functools.partial(_decode_kernel, n_chunks=n_chunks),
        grid_spec=grid_spec,
        out_shape=jax.ShapeDtypeStruct((bd, t_new, ATTN_W), F32),
        compiler_params=pltpu.CompilerParams(dimension_semantics=("arbitrary", "arbitrary"),
                                             vmem_limit_bytes=VMEM_LIMIT),
        name="decode_attn",
    )(page_table, q, kn, vn, bias, biasn, slope_rows, lamv, sg,
      *([ck] * DEC_P), *([cv] * DEC_P))


def _band(c0, c1):
    lo = (c0 // LRU_BW) * LRU_BW
    hi = ((c1 - 1) // LRU_BW + 1) * LRU_BW
    return lo // LANES * LANES, min(_round_up(hi, LANES), LRU_W)


def _lru_kernel(xr_ref, gr_ref, cs_ref, h0_ref, cw_ref, cb_ref, wa_ref, ba_ref, wx_ref, bx_ref, lam_ref,
                yl_ref, hl_ref, xp_s, h_s, a_s, gx_s, hs_s, *, rps, pad):
    tr = xr_ref.shape[1]
    hist = (LRU_CONV - 1) * rps

    @pl.when(pl.program_id(1) == 0)
    def _():
        xp_s[pad - hist:pad, :] = cs_ref[0]
        h_s[...] = h0_ref[0]

    xp_s[pad:pad + tr, :] = xr_ref[0]
    xc = cb_ref[...]
    for j in range(LRU_CONV):
        off = pad - (LRU_CONV - 1 - j) * rps
        xc = xc + cw_ref[j:j + 1, :] * xp_s[off:off + tr, :]
    xcb = xc.astype(BF16)

    def gate(w_ref, b_ref):
        outs = []
        for c0 in range(0, LRU_W, 2 * LANES):
            c1 = min(c0 + 2 * LANES, LRU_W)
            lo, hi = _band(c0, c1)
            outs.append(jnp.dot(xcb[:, lo:hi], w_ref[lo:hi, c0:c1], preferred_element_type=F32))
        return _sigmoid(jnp.concatenate(outs, axis=1) + b_ref[...])

    r = gate(wa_ref, ba_ref)
    i = gate(wx_ref, bx_ref)
    z = -lam_ref[...]
    softplus = jnp.maximum(z, 0.0) + jnp.log1p(jnp.exp(-jnp.abs(z)))
    log_a = (-LRU_C * r) * softplus
    a = jnp.exp(log_a)
    a_s[...] = a
    gx_s[...] = jnp.sqrt(-jnp.tanh(log_a) * (a * a + 1.0)) * (i * xc)

    def step(t, h):
        rows = pl.ds(pl.multiple_of(t * rps, rps), rps)
        h = a_s[rows, :] * h + gx_s[rows, :]
        hs_s[rows, :] = h
        return h

    h = lax.fori_loop(0, tr // rps, step, h_s[...], unroll=8)
    h_s[...] = h
    hl_ref[0] = h
    yl_ref[0] = (hs_s[...] * _gelu(gr_ref[0])).astype(BF16)
    xp_s[pad - hist:pad, :] = xp_s[pad + tr - hist:pad + tr, :]


def _lru(xr, gr, conv_state, h0, cw, cb, wa, ba, wx, bx, lam, *, rps, tr):
    nb, rows, _ = xr.shape
    pad = _round_up((LRU_CONV - 1) * rps, SUBLANES)
    tile = pl.BlockSpec((1, tr, LRU_W), lambda b, s: (b, s, 0))
    per_b = lambda shape: pl.BlockSpec(shape, lambda b, s: (b, 0, 0))
    return pl.pallas_call(
        functools.partial(_lru_kernel, rps=rps, pad=pad),
        grid=(nb, rows // tr),
        in_specs=[tile, tile, per_b((1, (LRU_CONV - 1) * rps, LRU_W)), per_b((1, rps, LRU_W)),
                  _resident((LRU_CONV, LRU_W)), _resident((1, LRU_W)),
                  _resident((LRU_W, LRU_W)), _resident((1, LRU_W)),
                  _resident((LRU_W, LRU_W)), _resident((1, LRU_W)), _resident((1, LRU_W))],
        out_specs=[tile, per_b((1, rps, LRU_W))],
        out_shape=[jax.ShapeDtypeStruct((nb, rows, LRU_W), BF16),
                   jax.ShapeDtypeStruct((nb, rps, LRU_W), F32)],
        scratch_shapes=[pltpu.VMEM((pad + tr, LRU_W), F32), pltpu.VMEM((rps, LRU_W), F32),
                        pltpu.VMEM((tr, LRU_W), F32), pltpu.VMEM((tr, LRU_W), F32),
                        pltpu.VMEM((tr, LRU_W), F32)],
        compiler_params=pltpu.CompilerParams(dimension_semantics=("arbitrary", "arbitrary"),
                                             vmem_limit_bytes=VMEM_LIMIT),
        name="rg_lru",
    )(xr, gr, conv_state, h0, cw, cb, wa, ba, wx, bx, lam)


FFN_TC = 512


def _post_kernel(x_ref, o_ref, yl_ref, ga_ref, gl_ref, fs_ref,
                 wba_ref, wbl_ref, wo_ref, g2_ref, wup_ref, fw_ref, fb_ref, wdn_ref,
                 y_ref, fso_ref, tail_s, xp_s, *, rps, pad):
    tr = x_ref.shape[1]
    hist = (FFN_CONV - 1) * rps

    @pl.when(pl.program_id(1) == 0)
    def _():
        tail_s[pad - hist:pad, :] = fs_ref[0]

    ya = jnp.dot(o_ref[0], wba_ref[...], preferred_element_type=F32)
    yr = jnp.dot(yl_ref[0], wbl_ref[...], preferred_element_type=F32)
    mix = _sigmoid(ga_ref[0]) * ya + _sigmoid(gl_ref[0]) * yr
    x1 = x_ref[0] + jnp.dot(mix.astype(BF16), wo_ref[...], preferred_element_type=F32)
    h2 = _rms(x1, g2_ref[...]).astype(BF16)

    def conv_cols(off, buf):
        up = jnp.dot(h2, wup_ref[:, off:off + FFN_TC], preferred_element_type=F32)
        buf[pad - hist:pad, :] = tail_s[pad - hist:pad, off:off + FFN_TC]
        buf[pad:pad + tr, :] = up
        tail_s[pad - hist:pad, off:off + FFN_TC] = up[tr - hist:tr, :]
        out = fb_ref[:, off:off + FFN_TC]
        for j in range(FFN_CONV):
            r0 = pad - (FFN_CONV - 1 - j) * rps
            out = out + fw_ref[j:j + 1, off:off + FFN_TC] * buf[r0:r0 + tr, :]
        return out

    acc = x1
    for n, c in enumerate(range(0, FF, FFN_TC)):
        u = conv_cols(c, xp_s.at[2 * n])
        g = conv_cols(FF + c, xp_s.at[2 * n + 1])
        act = (_gelu(g) * u).astype(BF16)
        acc = acc + jnp.dot(act, wdn_ref[c:c + FFN_TC, :], preferred_element_type=F32)
    y_ref[0] = acc
    fso_ref[0] = tail_s[pad - hist:pad, :]


def _post(x, o, yl, ga, gl, ffn_state, wba, wbl, wo, g2, wup, fw, fb, wdn, *, rps, tr):
    nb, rows, _ = x.shape
    pad = _round_up((FFN_CONV - 1) * rps, SUBLANES)
    tile = lambda w: pl.BlockSpec((1, tr, w), lambda b, s: (b, s, 0))
    state = pl.BlockSpec((1, (FFN_CONV - 1) * rps, 2 * FF), lambda b, s: (b, 0, 0))
    return pl.pallas_call(
        functools.partial(_post_kernel, rps=rps, pad=pad),
        grid=(nb, rows // tr),
        in_specs=[tile(D_MODEL), tile(ATTN_W), tile(LRU_W), tile(D_MODEL), tile(D_MODEL), state,
                  _resident((ATTN_W, D_MODEL)), _resident((LRU_W, D_MODEL)), _resident((D_MODEL, D_MODEL)),
                  _resident((1, D_MODEL)), _resident((D_MODEL, 2 * FF)), _resident((FFN_CONV, 2 * FF)),
                  _resident((1, 2 * FF)), _resident((FF, D_MODEL))],
        out_specs=[tile(D_MODEL), state],
        out_shape=[jax.ShapeDtypeStruct((nb, rows, D_MODEL), F32),
                   jax.ShapeDtypeStruct((nb, (FFN_CONV - 1) * rps, 2 * FF), F32)],
        scratch_shapes=[pltpu.VMEM((pad, 2 * FF), F32),
                        pltpu.VMEM((2 * FF // FFN_TC, pad + tr, FFN_TC), F32)],
        compiler_params=pltpu.CompilerParams(dimension_semantics=("arbitrary", "arbitrary"),
                                             vmem_limit_bytes=VMEM_LIMIT),
        name="out_ffn",
    )(x, o, yl, ga, gl, ffn_state, wba, wbl, wo, g2, wup, fw, fb, wdn)


def _block_diag(w):
    n, bw, _ = w.shape
    eye = jnp.eye(n, dtype=w.dtype)
    return jnp.einsum('nm,ncd->ncmd', eye, w).reshape(n * bw, n * bw)


SAMPLE_GROUPS = 4


def _time_major(a):
    b, t, c = a.shape
    g = SAMPLE_GROUPS
    return jnp.swapaxes(a.reshape(g, b // g, t, c), 1, 2).reshape(g, t * (b // g), c)


def _batch_major(a, t):
    g, rows, c = a.shape
    return jnp.swapaxes(a.reshape(g, t, rows // t, c), 1, 2).reshape(g * (rows // t), t, c)


def kernel(x_prompt, x_sample, cache_k, cache_v, page_table, state_lru_h, state_lru_conv, state_ffn_conv, norm1_g, w_in, q_norm_g, k_norm_g, lambda_q1, lambda_k1, lambda_q2, lambda_k2, subln_g, lru_conv_w, lru_conv_b, lru_wa, lru_ba, lru_wx, lru_bx, lru_lambda, w_branch_attn, w_branch_lru, w_out, norm2_g, w_up, ffn_conv_w, ffn_conv_b, w_down):
    assert w_in.shape[0] == 1, "single-layer step"
    b, s, _ = x_prompt.shape
    bd, t_new, _ = x_sample.shape
    row = lambda a: a.reshape(1, -1)

    w_in_bf = w_in[0].astype(BF16)
    g1 = row(norm1_g[0])
    qg = row(jnp.tile(q_norm_g[0], 2)) * (DK ** -0.5 * LOG2E)
    kg = row(jnp.tile(k_norm_g[0], 2))
    gid = np.arange(LANES) // DK
    gmat = jnp.asarray((gid[:, None] == gid[None, :]).astype(np.float32) / DK, BF16)
    lamv = jnp.stack([lambda_q1[0], lambda_k1[0], lambda_q2[0], lambda_k2[0]]).astype(F32)
    sg = row(subln_g[0])
    slopes = jnp.asarray(np.broadcast_to(
        (LOG2E * 2.0 ** (-8.0 * np.arange(1, N_HEADS + 1) / N_HEADS)).astype(np.float32)[:, None, None],
        (N_HEADS, 1, LANES)))
    wa = _block_diag(lru_wa[0]).astype(BF16)
    wx = _block_diag(lru_wx[0]).astype(BF16)
    lru_params = (lru_conv_w[0], row(lru_conv_b[0]), wa, row(lru_ba[0]), wx, row(lru_bx[0]),
                  row(lru_lambda[0]))
    post_params = (w_branch_attn[0].astype(BF16), w_branch_lru[0].astype(BF16), w_out[0].astype(BF16),
                   row(norm2_g[0]), w_up[0].astype(BF16), ffn_conv_w[0], row(ffn_conv_b[0]),
                   w_down[0].astype(BF16))

    qp, kbp, kp, vbp, vp, xrp, grp, gap, glp = _in_proj(
        x_prompt.reshape(b * s, D_MODEL), g1, w_in_bf, qg, kg, gmat)
    r3 = lambda a: a.reshape(b, s, a.shape[-1])
    o_p = _prompt_attn(r3(qp), r3(kbp), r3(vbp), slopes, lamv, sg)
    xrp3 = r3(xrp)
    yl_p, h_p = _lru(xrp3, r3(grp), jnp.zeros((b, LRU_CONV - 1, LRU_W), F32), jnp.zeros((b, 1, LRU_W), F32),
                     *lru_params, rps=1, tr=256)
    y_p, fs_p = _post(x_prompt, o_p, yl_p, r3(gap), r3(glp), jnp.zeros((b, FFN_CONV - 1, 2 * FF), F32),
                      *post_params, rps=1, tr=256)

    qs_, kbs, ks, vbs, vs, xrs, grs, gas, gls = _in_proj(
        x_sample.reshape(bd * t_new, D_MODEL), g1, w_in_bf, qg, kg, gmat)
    del kbs, vbs
    o_s = _decode_attn(page_table, qs_.astype(F32).reshape(bd, t_new, ATTN_W),
                       ks.reshape(bd, t_new * N_HEADS, LANES), vs.reshape(bd, t_new * N_HEADS, LANES),
                       cache_k[0], cache_v[0], lamv, sg)
    d3 = lambda a: a.reshape(bd, t_new, a.shape[-1])
    bg = bd // SAMPLE_GROUPS
    xrs_t = _time_major(d3(xrs))
    yl_s, h_s = _lru(xrs_t, _time_major(d3(grs)), _time_major(state_lru_conv[0]),
                     state_lru_h[0].reshape(SAMPLE_GROUPS, bg, LRU_W), *lru_params, rps=bg, tr=bg * t_new)
    y_s, fs_s = _post(_time_major(x_sample), _time_major(o_s.astype(BF16)), yl_s,
                      _time_major(d3(gas)), _time_major(d3(gls)),
                      _time_major(state_ffn_conv[0]), *post_params, rps=bg, tr=bg * t_new)

    hd = lambda a, n, t: a.reshape(1, n, t, N_HEADS, LANES)
    return (y_p, _batch_major(y_s, t_new),
            hd(kp, b, s), hd(vp, b, s),
            h_p.reshape(1, b, LRU_W), xrp3[None, :, s - (LRU_CONV - 1):, :], fs_p[None],
            hd(ks, bd, t_new), hd(vs, bd, t_new),
            h_s.reshape(1, bd, LRU_W),
            _batch_major(xrs_t[:, (t_new - LRU_CONV + 1) * bg:, :], LRU_CONV - 1)[None],
            _batch_major(fs_s, FFN_CONV - 1)[None])
```

```python
import functools
import math

import numpy as np
import jax
import jax.numpy as jnp
from jax import lax
from jax.experimental import pallas as pl
from jax.experimental.pallas import tpu as pltpu

F32 = jnp.float32
BF16 = jnp.bfloat16

D_MODEL = 1024
N_HEADS = 8
DK = D_MODEL // N_HEADS // 2
DV = 2 * DK
ATTN_W = N_HEADS * DV
LRU_BLOCKS = 8
LRU_W = ((4 * D_MODEL // 3 + 127) // 128) * 128
LRU_BW = LRU_W // LRU_BLOCKS
LRU_CONV = 4
LRU_C = 8.0
FF = 3 * D_MODEL
FFN_CONV = 3
PAGE_SIZE = 128
EPS = 1e-6
LAM_INIT = 0.8 - 0.6 * math.exp(-0.3 * 0)
LOG2E = math.log2(math.e)
GELU_C = math.sqrt(2.0 / math.pi)

LANES = 128
SUBLANES = 8
NEG = -1e30
VMEM_LIMIT = 56 * 1024 * 1024

QKV_W = N_HEADS * 2 * DK
OFF_Q, OFF_K, OFF_V = 0, QKV_W, 2 * QKV_W
OFF_XR = 3 * QKV_W
OFF_GR = OFF_XR + LRU_W
OFF_GA = OFF_GR + LRU_W
OFF_GL = OFF_GA + D_MODEL
IN_W = OFF_GL + D_MODEL

NT_DIMS = (((1,), (1,)), ((), ()))


def _round_up(x, m):
    return (x + m - 1) // m * m


def _resident(shape):
    nd = len(shape)
    return pl.BlockSpec(shape, lambda *_: (0,) * nd, pipeline_mode=pl.Buffered(1))


def _rms(x, g):
    ms = jnp.mean(x * x, axis=-1, keepdims=True)
    return x * lax.rsqrt(ms + EPS) * g


def _sigmoid(x):
    return 0.5 * jnp.tanh(0.5 * x) + 0.5


def _gelu(x):
    hx = 0.5 * x
    return hx + hx * jnp.tanh(x * (GELU_C + (GELU_C * 0.044715) * (x * x)))


def _lam(lamv_ref):
    lv = lamv_ref[...]
    s1 = jnp.sum(lv[0:1] * lv[1:2], axis=-1, keepdims=True)
    s2 = jnp.sum(lv[2:3] * lv[3:4], axis=-1, keepdims=True)
    return jnp.exp(s1) - jnp.exp(s2) + LAM_INIT


IN_PROJ_TM = 256
IN_PROJ_TN = 512


def _in_proj_kernel(x_ref, g1_ref, w_ref, qg_ref, kg_ref, gm_ref,
                    q_ref, kb_ref, k_ref, vb_ref, v_ref, xr_ref, gr_ref, ga_ref, gl_ref):
    h = _rms(x_ref[...], g1_ref[...]).astype(BF16)

    def proj(c0, c1):
        return jnp.dot(h, w_ref[:, c0:c1], preferred_element_type=F32)

    def group_norm(z, g):
        outs = []
        for j in range(z.shape[1] // LANES):
            zj = z[:, j * LANES:(j + 1) * LANES]
            ms = jnp.dot((zj * zj).astype(BF16), gm_ref[...], preferred_element_type=F32)
            outs.append(zj * lax.rsqrt(ms + EPS) * g)
        return jnp.concatenate(outs, axis=1)

    tn = IN_PROJ_TN
    for c in range(0, QKV_W, tn):
        q = group_norm(proj(OFF_Q + c, OFF_Q + c + tn), qg_ref[...])
        q_ref[:, c:c + tn] = q.astype(BF16)
        k = group_norm(proj(OFF_K + c, OFF_K + c + tn), kg_ref[...])
        k_ref[:, c:c + tn] = k
        kb_ref[:, c:c + tn] = k.astype(BF16)
        v = proj(OFF_V + c, OFF_V + c + tn)
        v_ref[:, c:c + tn] = v
        vb_ref[:, c:c + tn] = v.astype(BF16)
        ga_ref[:, c:c + tn] = proj(OFF_GA + c, OFF_GA + c + tn)
        gl_ref[:, c:c + tn] = proj(OFF_GL + c, OFF_GL + c + tn)
    for c in range(0, LRU_W, tn):
        c1 = min(c + tn, LRU_W)
        xr_ref[:, c:c1] = proj(OFF_XR + c, OFF_XR + c1)
        gr_ref[:, c:c1] = proj(OFF_GR + c, OFF_GR + c1)


def _in_proj(x2d, g1, w_in_bf, qg, kg, gmat):
    t = x2d.shape[0]
    tm = IN_PROJ_TM
    row = lambda w: pl.BlockSpec((tm, w), lambda i: (i, 0))
    out_shapes = [
        jax.ShapeDtypeStruct((t, QKV_W), BF16),
        jax.ShapeDtypeStruct((t, QKV_W), BF16),
        jax.ShapeDtypeStruct((t, QKV_W), F32),
        jax.ShapeDtypeStruct((t, QKV_W), BF16),
        jax.ShapeDtypeStruct((t, QKV_W), F32),
        jax.ShapeDtypeStruct((t, LRU_W), F32),
        jax.ShapeDtypeStruct((t, LRU_W), F32),
        jax.ShapeDtypeStruct((t, D_MODEL), F32),
        jax.ShapeDtypeStruct((t, D_MODEL), F32),
    ]
    return pl.pallas_call(
        _in_proj_kernel,
        grid=(t // tm,),
        in_specs=[row(D_MODEL), _resident((1, D_MODEL)), _resident((D_MODEL, IN_W)),
                  _resident((1, LANES)), _resident((1, LANES)), _resident((LANES, LANES))],
        out_specs=[row(s.shape[1]) for s in out_shapes],
        out_shape=out_shapes,
        compiler_params=pltpu.CompilerParams(dimension_semantics=("arbitrary",),
                                             vmem_limit_bytes=VMEM_LIMIT),
        name="in_proj",
    )(x2d, g1, w_in_bf, qg, kg, gmat)


ATT_BLK = 512
ATT_CHAIN = 256


def _prompt_attn_kernel(q_ref, k_ref, v_ref, slope_ref, lamv_ref, sg_ref, o_ref,
                        q2_s, vx_s, m_s, acc_s):
    blk = ATT_BLK
    qi = pl.program_id(2)

    @pl.when(qi == 0)
    def _():
        vx_s[:, :DV] = v_ref[0]
        vx_s[:, DV:] = jnp.ones((vx_s.shape[0], LANES), BF16)

    q = q_ref[0]
    lane = lax.broadcasted_iota(jnp.int32, q.shape, 1)
    q2_s[:blk, :] = jnp.where(lane < DK, q, jnp.zeros_like(q))
    q2_s[blk:, :] = jnp.where(lane >= DK, q, jnp.zeros_like(q))
    m_s[...] = jnp.full(m_s.shape, NEG, F32)
    acc_s[...] = jnp.zeros(acc_s.shape, F32)

    slope = slope_ref[0][:, :1]
    col_f = lax.broadcasted_iota(jnp.int32, (1, blk), 1).astype(F32)

    def block(ki, masked):
        start = pl.multiple_of(ki * blk, blk)
        k = k_ref[0, pl.ds(start, blk), :]
        vx = vx_s[pl.ds(start, blk), :]
        bias = slope * (col_f + ((ki - qi) * blk).astype(F32))
        if masked:
            row = lax.broadcasted_iota(jnp.int32, (blk, blk), 0)
            col = lax.broadcasted_iota(jnp.int32, (blk, blk), 1)
            keep = col <= row
        halves = tuple(slice(r, r + ATT_CHAIN) for r in range(0, 2 * blk, ATT_CHAIN))
        scores = []
        for n, rows in enumerate(halves):
            s = lax.dot_general(q2_s[rows, :], k, NT_DIMS, preferred_element_type=F32) + bias
            if masked:
                r0 = (n * ATT_CHAIN) % blk
                s = jnp.where(keep[r0:r0 + ATT_CHAIN], s, NEG)
            scores.append(s)
        probs = []
        for rows, s in zip(halves, scores):
            m_prev = m_s[rows, :]
            m_next = jnp.maximum(m_prev, jnp.max(s, axis=-1, keepdims=True))
            probs.append((jnp.exp2(s - jnp.tile(m_next, (1, blk // LANES))).astype(BF16),
                          jnp.exp2(m_prev - m_next)))
            m_s[rows, :] = m_next
        for rows, (p, alpha) in zip(halves, probs):
            pv = jnp.dot(p, vx, preferred_element_type=F32)
            acc_s[rows, :] = jnp.tile(alpha, (1, 2)) * acc_s[rows, :] + pv

    def body(ki, carry):
        block(ki, False)
        return carry

    lax.fori_loop(0, qi, body, 0)
    block(qi, True)

    lam = _lam(lamv_ref)
    acc = acc_s[...]
    o1 = acc[:blk, :DV] / acc[:blk, DV:]
    o2 = acc[blk:, :DV] / acc[blk:, DV:]
    o_ref[0] = (_rms(o1 - lam * o2, sg_ref[...]) * (1.0 - LAM_INIT)).astype(BF16)


def _prompt_attn(q, k, v, slopes, lamv, sg):
    b, s, _ = q.shape
    blk = ATT_BLK
    seq = pl.BlockSpec((1, s, LANES), lambda bb, h, i: (bb, 0, h))
    tile = pl.BlockSpec((1, blk, LANES), lambda bb, h, i: (bb, i, h))
    return pl.pallas_call(
        _prompt_attn_kernel,
        grid=(b, N_HEADS, s // blk),
        in_specs=[tile, seq, seq,
                  pl.BlockSpec((1, 1, LANES), lambda bb, h, i: (h, 0, 0)),
                  pl.BlockSpec((4, DK), lambda bb, h, i: (0, 0)),
                  pl.BlockSpec((1, DV), lambda bb, h, i: (0, 0))],
        out_specs=tile,
        out_shape=jax.ShapeDtypeStruct((b, s, ATTN_W), BF16),
        scratch_shapes=[pltpu.VMEM((2 * blk, LANES), BF16), pltpu.VMEM((s, 2 * LANES), BF16),
                        pltpu.VMEM((2 * blk, LANES), F32), pltpu.VMEM((2 * blk, 2 * LANES), F32)],
        compiler_params=pltpu.CompilerParams(
            dimension_semantics=("arbitrary", "arbitrary", "arbitrary"),
            vmem_limit_bytes=VMEM_LIMIT),
        name="prompt_attn",
    )(q, k, v, slopes, lamv, sg)


DEC_P = 16
PAIR_ROWS = 2 * PAGE_SIZE
QP_ROWS = 32
N_PAIRS = N_HEADS // 2


def _decode_kernel(pt_ref, q_ref, kn_ref, vn_ref, bias_ref, biasn_ref, slope_ref, lamv_ref, sg_ref,
                   *rest, n_chunks):
    del pt_ref
    kp = rest[:DEC_P]
    vp = rest[DEC_P:2 * DEC_P]
    o_ref = rest[2 * DEC_P]
    qs, m_s, l_s, acc_s = rest[2 * DEC_P + 1:]
    c = pl.program_id(1)
    t_new = q_ref.shape[1]

    @pl.when(c == 0)
    def _():
        q = q_ref[0]
        lane = lax.broadcasted_iota(jnp.int32, (t_new, LANES), 1)
        for p in range(N_PAIRS):
            parts = []
            for hs in range(2):
                h = p + N_PAIRS * hs
                qh = q[:, h * LANES:(h + 1) * LANES]
                parts.append(jnp.where(lane < DK, qh, 0.0))
                parts.append(jnp.where(lane >= DK, qh, 0.0))
            qs[p] = jnp.concatenate(parts, axis=0).astype(BF16)
        m_s[...] = jnp.full(m_s.shape, NEG, F32)
        l_s[...] = jnp.zeros(l_s.shape, F32)
        acc_s[...] = jnp.zeros(acc_s.shape, F32)

    def pair_rows(refs, p, n):
        return jnp.concatenate([r[0, pl.ds(p, n, stride=N_PAIRS), :] for r in refs], axis=0).astype(BF16)

    def softmax_update(p, u, shift_back):
        m_old = m_s[p]
        m_new = jnp.maximum(m_old, jnp.max(u, axis=-1, keepdims=True) + shift_back)
        ref = m_new - shift_back
        n = u.shape[1]
        pe = jnp.exp2(u - (jnp.tile(ref, (1, n // LANES)) if n >= LANES else ref[:, :n]))
        alpha = jnp.exp2(m_old - m_new)
        l_s[p] = alpha * l_s[p] + jnp.sum(pe, axis=-1, keepdims=True)
        m_s[p] = m_new
        return pe.astype(BF16), alpha

    chunk_off = ((c + 1 - n_chunks) * (DEC_P * PAGE_SIZE)).astype(F32)
    scores = [lax.dot_general(qs[p], pair_rows(kp, p, PAIR_ROWS), NT_DIMS, preferred_element_type=F32)
              + bias_ref[p] for p in range(N_PAIRS)]
    probs = [softmax_update(p, scores[p], slope_ref[p] * chunk_off) for p in range(N_PAIRS)]
    for p in range(N_PAIRS):
        pb, alpha = probs[p]
        acc_s[p] = alpha * acc_s[p] + jnp.dot(pb, pair_rows(vp, p, PAIR_ROWS), preferred_element_type=F32)

    @pl.when(c == n_chunks - 1)
    def _():
        lam = _lam(lamv_ref)
        for p in range(N_PAIRS):
            kn = pair_rows([kn_ref], p, 2 * t_new)
            vn = pair_rows([vn_ref], p, 2 * t_new)
            u = lax.dot_general(qs[p], kn, NT_DIMS, preferred_element_type=F32) + biasn_ref[p]
            pb, alpha = softmax_update(p, u, 0.0)
            acc = alpha * acc_s[p] + jnp.dot(pb, vn, preferred_element_type=F32)
            out = acc / l_s[p]
            for hs in range(2):
                h = p + N_PAIRS * hs
                r0 = hs * 2 * t_new
                a = out[r0:r0 + t_new] - lam * out[r0 + t_new:r0 + 2 * t_new]
                o_ref[0, :, h * LANES:(h + 1) * LANES] = _rms(a, sg_ref[...]) * (1.0 - LAM_INIT)


def _decode_tables(t_new):
    slopes = LOG2E * 2.0 ** (-8.0 * np.arange(1, N_HEADS + 1) / N_HEADS)
    rows = np.arange(QP_ROWS)
    row_hs, row_t = rows // (2 * t_new), rows % t_new
    lane = np.arange(PAIR_ROWS * DEC_P)
    lane_hs = lane % 2
    lane_key = (lane % PAIR_ROWS) // 2 + (lane // PAIR_ROWS) * PAGE_SIZE
    lane_n = np.arange(2 * t_new)
    bias = np.zeros((N_PAIRS, QP_ROWS, lane.size), np.float32)
    biasn = np.zeros((N_PAIRS, QP_ROWS, lane_n.size), np.float32)
    slope_rows = np.zeros((N_PAIRS, QP_ROWS, LANES), np.float32)
    for p in range(N_PAIRS):
        sl = slopes[p + N_PAIRS * row_hs]
        slope_rows[p] = sl[:, None]
        match = row_hs[:, None] == lane_hs[None, :]
        rel = lane_key[None, :] - DEC_P * PAGE_SIZE
        bias[p] = np.where(match, sl[:, None] * rel, NEG)
        match_n = (row_hs[:, None] == (lane_n % 2)[None, :]) & ((lane_n // 2)[None, :] <= row_t[:, None])
        biasn[p] = np.where(match_n, sl[:, None] * (lane_n // 2)[None, :], NEG)
    return jnp.asarray(bias), jnp.asarray(biasn), jnp.asarray(slope_rows)


def _decode_attn(page_table, q, kn, vn, cache_k, cache_v, lamv, sg):
    bd, t_new, _ = q.shape
    n_pages = page_table.shape[1]
    n_chunks = n_pages // DEC_P
    n_pool = cache_k.shape[0]
    bias, biasn, slope_rows = _decode_tables(t_new)
    page_rows = PAGE_SIZE * N_HEADS

    def page_spec(i):
        return pl.BlockSpec((1, page_rows, LANES), lambda b, c, pt, i=i: (pt[b, c * DEC_P + i], 0, 0))

    const = lambda shape: pl.BlockSpec(shape, lambda b, c, pt: (0,) * len(shape))
    per_b = lambda shape: pl.BlockSpec(shape, lambda b, c, pt: (b,) + (0,) * (len(shape) - 1))
    grid_spec = pltpu.PrefetchScalarGridSpec(
        num_scalar_prefetch=1,
        grid=(bd, n_chunks),
        in_specs=[per_b((1, t_new, ATTN_W)), per_b((1, t_new * N_HEADS, LANES)),
                  per_b((1, t_new * N_HEADS, LANES)),
                  const(bias.shape), const(biasn.shape), const(slope_rows.shape),
                  const((4, DK)), const((1, DV))]
                 + [page_spec(i) for i in range(DEC_P)] + [page_spec(i) for i in range(DEC_P)],
        out_specs=per_b((1, t_new, ATTN_W)),
        scratch_shapes=[pltpu.VMEM((N_PAIRS, QP_ROWS, LANES), BF16), pltpu.VMEM((N_PAIRS, QP_ROWS, LANES), F32),
                        pltpu.VMEM((N_PAIRS, QP_ROWS, LANES), F32), pltpu.VMEM((N_PAIRS, QP_ROWS, DV), F32)],
    )
    ck = cache_k.reshape(n_pool, page_rows, LANES)
    cv = cache_v.reshape(n_pool, page_rows, LANES)
    return pl.pallas_call(
        functools.partial(_decode_kernel, n_chunks=n_chunks),
        grid_spec=grid_spec,
        out_shape=jax.ShapeDtypeStruct((bd, t_new, ATTN_W), F32),
        compiler_params=pltpu.CompilerParams(dimension_semantics=("arbitrary", "arbitrary"),
                                             vmem_limit_bytes=VMEM_LIMIT),
        name="decode_attn",
    )(page_table, q, kn, vn, bias, biasn, slope_rows, lamv, sg,
      *([ck] * DEC_P), *([cv] * DEC_P))


def _band(c0, c1):
    lo = (c0 // LRU_BW) * LRU_BW
    hi = ((c1 - 1) // LRU_BW + 1) * LRU_BW
    return lo // LANES * LANES, min(_round_up(hi, LANES), LRU_W)


def _lru_kernel(xr_ref, gr_ref, cs_ref, h0_ref, cw_ref, cb_ref, wa_ref, ba_ref, wx_ref, bx_ref, lam_ref,
                yl_ref, hl_ref, xp_s, h_s, a_s, gx_s, hs_s, *, rps, pad):
    tr = xr_ref.shape[1]
    hist = (LRU_CONV - 1) * rps

    @pl.when(pl.program_id(1) == 0)
    def _():
        xp_s[pad - hist:pad, :] = cs_ref[0]
        h_s[...] = h0_ref[0]

    xp_s[pad:pad + tr, :] = xr_ref[0]
    xc = cb_ref[...]
    for j in range(LRU_CONV):
        off = pad - (LRU_CONV - 1 - j) * rps
        xc = xc + cw_ref[j:j + 1, :] * xp_s[off:off + tr, :]
    xcb = xc.astype(BF16)

    def gate(w_ref, b_ref):
        outs = []
        for c0 in range(0, LRU_W, 2 * LANES):
            c1 = min(c0 + 2 * LANES, LRU_W)
            lo, hi = _band(c0, c1)
            outs.append(jnp.dot(xcb[:, lo:hi], w_ref[lo:hi, c0:c1], preferred_element_type=F32))
        return _sigmoid(jnp.concatenate(outs, axis=1) + b_ref[...])

    r = gate(wa_ref, ba_ref)
    i = gate(wx_ref, bx_ref)
    z = -lam_ref[...]
    softplus = jnp.maximum(z, 0.0) + jnp.log1p(jnp.exp(-jnp.abs(z)))
    log_a = (-LRU_C * r) * softplus
    a = jnp.exp(log_a)
    a_s[...] = a
    gx_s[...] = jnp.sqrt(-jnp.tanh(log_a) * (a * a + 1.0)) * (i * xc)

    def step(t, h):
        rows = pl.ds(pl.multiple_of(t * rps, rps), rps)
        h = a_s[rows, :] * h + gx_s[rows, :]
        hs_s[rows, :] = h
        return h

    h = lax.fori_loop(0, tr // rps, step, h_s[...], unroll=8)
    h_s[...] = h
    hl_ref[0] = h
    yl_ref[0] = (hs_s[...] * _gelu(gr_ref[0])).astype(BF16)
    xp_s[pad - hist:pad, :] = xp_s[pad + tr - hist:pad + tr, :]


def _lru(xr, gr, conv_state, h0, cw, cb, wa, ba, wx, bx, lam, *, rps, tr):
    nb, rows, _ = xr.shape
    pad = _round_up((LRU_CONV - 1) * rps, SUBLANES)
    tile = pl.BlockSpec((1, tr, LRU_W), lambda b, s: (b, s, 0))
    per_b = lambda shape: pl.BlockSpec(shape, lambda b, s: (b, 0, 0))
    return pl.pallas_call(
        functools.partial(_lru_kernel, rps=rps, pad=pad),
        grid=(nb, rows // tr),
        in_specs=[tile, tile, per_b((1, (LRU_CONV - 1) * rps, LRU_W)), per_b((1, rps, LRU_W)),
                  _resident((LRU_CONV, LRU_W)), _resident((1, LRU_W)),
                  _resident((LRU_W, LRU_W)), _resident((1, LRU_W)),
                  _resident((LRU_W, LRU_W)), _resident((1, LRU_W)), _resident((1, LRU_W))],
        out_specs=[tile, per_b((1, rps, LRU_W))],
        out_shape=[jax.ShapeDtypeStruct((nb, rows, LRU_W), BF16),
                   jax.ShapeDtypeStruct((nb, rps, LRU_W), F32)],
        scratch_shapes=[pltpu.VMEM((pad + tr, LRU_W), F32), pltpu.VMEM((rps, LRU_W), F32),
                        pltpu.VMEM((tr, LRU_W), F32), pltpu.VMEM((tr, LRU_W), F32),
                        pltpu.VMEM((tr, LRU_W), F32)],
        compiler_params=pltpu.CompilerParams(dimension_semantics=("arbitrary", "arbitrary"),
                                             vmem_limit_bytes=VMEM_LIMIT),
        name="rg_lru",
    )(xr, gr, conv_state, h0, cw, cb, wa, ba, wx, bx, lam)


FFN_TC = 512


def _post_kernel(x_ref, o_ref, yl_ref, ga_ref, gl_ref, fs_ref,
                 wba_ref, wbl_ref, wo_ref, g2_ref, wup_ref, fw_ref, fb_ref, wdn_ref,
                 y_ref, fso_ref, tail_s, xp_s, *, rps, pad):
    tr = x_ref.shape[1]
    hist = (FFN_CONV - 1) * rps

    @pl.when(pl.program_id(1) == 0)
    def _():
        tail_s[pad - hist:pad, :] = fs_ref[0]

    ya = jnp.dot(o_ref[0], wba_ref[...], preferred_element_type=F32)
    yr = jnp.dot(yl_ref[0], wbl_ref[...], preferred_element_type=F32)
    mix = _sigmoid(ga_ref[0]) * ya + _sigmoid(gl_ref[0]) * yr
    x1 = x_ref[0] + jnp.dot(mix.astype(BF16), wo_ref[...], preferred_element_type=F32)
    h2 = _rms(x1, g2_ref[...]).astype(BF16)

    def up_proj(c):
        return [jnp.dot(h2, wup_ref[:, off:off + FFN_TC], preferred_element_type=F32) for off in (c, FF + c)]

    def conv_cols(up, off, buf):
        buf[pad - hist:pad, :] = tail_s[pad - hist:pad, off:off + FFN_TC]
        buf[pad:pad + tr, :] = up
        tail_s[pad - hist:pad, off:off + FFN_TC] = up[tr - hist:tr, :]
        out = fb_ref[:, off:off + FFN_TC]
        for j in range(FFN_CONV):
            r0 = pad - (FFN_CONV - 1 - j) * rps
            out = out + fw_ref[j:j + 1, off:off + FFN_TC] * buf[r0:r0 + tr, :]
        return out

    acc = x1
    chunks = list(range(0, FF, FFN_TC))
    ups_next = up_proj(chunks[0])
    for n, c in enumerate(chunks):
        ups = ups_next
        if n + 1 < len(chunks):
            ups_next = up_proj(chunks[n + 1])
        u = conv_cols(ups[0], c, xp_s.at[2 * n])
        g = conv_cols(ups[1], FF + c, xp_s.at[2 * n + 1])
        act = (_gelu(g) * u).astype(BF16)
        acc = acc + jnp.dot(act, wdn_ref[c:c + FFN_TC, :], preferred_element_type=F32)
    y_ref[0] = acc
    fso_ref[0] = tail_s[pad - hist:pad, :]


def _post(x, o, yl, ga, gl, ffn_state, wba, wbl, wo, g2, wup, fw, fb, wdn, *, rps, tr):
    nb, rows, _ = x.shape
    pad = _round_up((FFN_CONV - 1) * rps, SUBLANES)
    tile = lambda w: pl.BlockSpec((1, tr, w), lambda b, s: (b, s, 0))
    state = pl.BlockSpec((1, (FFN_CONV - 1) * rps, 2 * FF), lambda b, s: (b, 0, 0))
    return pl.pallas_call(
        functools.partial(_post_kernel, rps=rps, pad=pad),
        grid=(nb, rows // tr),
        in_specs=[tile(D_MODEL), tile(ATTN_W), tile(LRU_W), tile(D_MODEL), tile(D_MODEL), state,
                  _resident((ATTN_W, D_MODEL)), _resident((LRU_W, D_MODEL)), _resident((D_MODEL, D_MODEL)),
                  _resident((1, D_MODEL)), _resident((D_MODEL, 2 * FF)), _resident((FFN_CONV, 2 * FF)),
                  _resident((1, 2 * FF)), _resident((FF, D_MODEL))],
        out_specs=[tile(D_MODEL), state],
        out_shape=[jax.ShapeDtypeStruct((nb, rows, D_MODEL), F32),
                   jax.ShapeDtypeStruct((nb, (FFN_CONV - 1) * rps, 2 * FF), F32)],
        scratch_shapes=[pltpu.VMEM((pad, 2 * FF), F32),
                        pltpu.VMEM((2 * FF // FFN_TC, pad + tr, FFN_TC), F32)],
        compiler_params=pltpu.CompilerParams(dimension_semantics=("arbitrary", "arbitrary"),
                                             vmem_limit_bytes=VMEM_LIMIT),
        name="out_ffn",
    )(x, o, yl, ga, gl, ffn_state, wba, wbl, wo, g2, wup, fw, fb, wdn)


def _block_diag(w):
    n, bw, _ = w.shape
    width = n * bw
    blocks = [jnp.pad(w[i], ((i * bw, width - (i + 1) * bw),) * 2) for i in range(n)]
    return functools.reduce(jnp.add, blocks)


SAMPLE_GROUPS = 4


def _time_major(a):
    b, t, c = a.shape
    g = SAMPLE_GROUPS
    return jnp.swapaxes(a.reshape(g, b // g, t, c), 1, 2).reshape(g, t * (b // g), c)


def _batch_major(a, t):
    g, rows, c = a.shape
    return jnp.swapaxes(a.reshape(g, t, rows // t, c), 1, 2).reshape(g * (rows // t), t, c)


def kernel(x_prompt, x_sample, cache_k, cache_v, page_table, state_lru_h, state_lru_conv, state_ffn_conv, norm1_g, w_in, q_norm_g, k_norm_g, lambda_q1, lambda_k1, lambda_q2, lambda_k2, subln_g, lru_conv_w, lru_conv_b, lru_wa, lru_ba, lru_wx, lru_bx, lru_lambda, w_branch_attn, w_branch_lru, w_out, norm2_g, w_up, ffn_conv_w, ffn_conv_b, w_down):
    assert w_in.shape[0] == 1, "single-layer step"
    b, s, _ = x_prompt.shape
    bd, t_new, _ = x_sample.shape
    row = lambda a: a.reshape(1, -1)

    w_in_bf = w_in[0].astype(BF16)
    g1 = row(norm1_g[0])
    qg = row(jnp.tile(q_norm_g[0], 2)) * (DK ** -0.5 * LOG2E)
    kg = row(jnp.tile(k_norm_g[0], 2))
    gid = np.arange(LANES) // DK
    gmat = jnp.asarray((gid[:, None] == gid[None, :]).astype(np.float32) / DK, BF16)
    lamv = jnp.stack([lambda_q1[0], lambda_k1[0], lambda_q2[0], lambda_k2[0]]).astype(F32)
    sg = row(subln_g[0])
    slopes = jnp.asarray(np.broadcast_to(
        (LOG2E * 2.0 ** (-8.0 * np.arange(1, N_HEADS + 1) / N_HEADS)).astype(np.float32)[:, None, None],
        (N_HEADS, 1, LANES)))
    wa = _block_diag(lru_wa[0]).astype(BF16)
    wx = _block_diag(lru_wx[0]).astype(BF16)
    lru_params = (lru_conv_w[0], row(lru_conv_b[0]), wa, row(lru_ba[0]), wx, row(lru_bx[0]),
                  row(lru_lambda[0]))
    post_params = (w_branch_attn[0].astype(BF16), w_branch_lru[0].astype(BF16), w_out[0].astype(BF16),
                   row(norm2_g[0]), w_up[0].astype(BF16), ffn_conv_w[0], row(ffn_conv_b[0]),
                   w_down[0].astype(BF16))

    qp, kbp, kp, vbp, vp, xrp, grp, gap, glp = _in_proj(
        x_prompt.reshape(b * s, D_MODEL), g1, w_in_bf, qg, kg, gmat)
    r3 = lambda a: a.reshape(b, s, a.shape[-1])
    o_p = _prompt_attn(r3(qp), r3(kbp), r3(vbp), slopes, lamv, sg)
    xrp3 = r3(xrp)
    yl_p, h_p = _lru(xrp3, r3(grp), jnp.zeros((b, LRU_CONV - 1, LRU_W), F32), jnp.zeros((b, 1, LRU_W), F32),
                     *lru_params, rps=1, tr=256)
    y_p, fs_p = _post(x_prompt, o_p, yl_p, r3(gap), r3(glp), jnp.zeros((b, FFN_CONV - 1, 2 * FF), F32),
                      *post_params, rps=1, tr=256)

    qs_, kbs, ks, vbs, vs, xrs, grs, gas, gls = _in_proj(
        x_sample.reshape(bd * t_new, D_MODEL), g1, w_in_bf, qg, kg, gmat)
    del kbs, vbs
    o_s = _decode_attn(page_table, qs_.astype(F32).reshape(bd, t_new, ATTN_W),
                       ks.reshape(bd, t_new * N_HEADS, LANES), vs.reshape(bd, t_new * N_HEADS, LANES),
                       cache_k[0], cache_v[0], lamv, sg)
    d3 = lambda a: a.reshape(bd, t_new, a.shape[-1])
    bg = bd // SAMPLE_GROUPS
    xrs_t = _time_major(d3(xrs))
    yl_s, h_s = _lru(xrs_t, _time_major(d3(grs)), _time_major(state_lru_conv[0]),
                     state_lru_h[0].reshape(SAMPLE_GROUPS, bg, LRU_W), *lru_params, rps=bg, tr=bg * t_new)
    y_s, fs_s = _post(_time_major(x_sample), _time_major(o_s.astype(BF16)), yl_s,
                      _time_major(d3(gas)), _time_major(d3(gls)),
                      _time_major(state_ffn_conv[0]), *post_params, rps=bg, tr=bg * t_new)

    hd = lambda a, n, t: a.reshape(1, n, t, N_HEADS, LANES)
    return (y_p, _batch_major(y_s, t_new),
            hd(kp, b, s), hd(vp, b, s),
            h_p.reshape(1, b, LRU_W), xrp3[None, :, s - (LRU_CONV - 1):, :], fs_p[None],
            hd(ks, bd, t_new), hd(vs, bd, t_new),
            h_s.reshape(1, bd, LRU_W),
            _batch_major(xrs_t[:, (t_new - LRU_CONV + 1) * bg:, :], LRU_CONV - 1)[None],
            _batch_major(fs_s, FFN_CONV - 1)[None])
```

```python
import functools
import math

import numpy as np
import jax
import jax.numpy as jnp
from jax import lax
from jax.experimental import pallas as pl
from jax.experimental.pallas import tpu as pltpu

F32 = jnp.float32
BF16 = jnp.bfloat16

D_MODEL = 1024
N_HEADS = 8
DK = D_MODEL // N_HEADS // 2
DV = 2 * DK
ATTN_W = N_HEADS * DV
LRU_BLOCKS = 8
LRU_W = ((4 * D_MODEL // 3 + 127) // 128) * 128
LRU_BW = LRU_W // LRU_BLOCKS
LRU_CONV = 4
LRU_C = 8.0
FF = 3 * D_MODEL
FFN_CONV = 3
PAGE_SIZE = 128
EPS = 1e-6
LAM_INIT = 0.8 - 0.6 * math.exp(-0.3 * 0)
LOG2E = math.log2(math.e)
GELU_C = math.sqrt(2.0 / math.pi)

LANES = 128
SUBLANES = 8
NEG = -1e30
VMEM_LIMIT = 56 * 1024 * 1024

QKV_W = N_HEADS * 2 * DK
OFF_Q, OFF_K, OFF_V = 0, QKV_W, 2 * QKV_W
OFF_XR = 3 * QKV_W
OFF_GR = OFF_XR + LRU_W
OFF_GA = OFF_GR + LRU_W
OFF_GL = OFF_GA + D_MODEL
IN_W = OFF_GL + D_MODEL

NT_DIMS = (((1,), (1,)), ((), ()))


def _round_up(x, m):
    return (x + m - 1) // m * m


def _resident(shape):
    nd = len(shape)
    return pl.BlockSpec(shape, lambda *_: (0,) * nd, pipeline_mode=pl.Buffered(1))


def _rms(x, g):
    ms = jnp.mean(x * x, axis=-1, keepdims=True)
    return x * lax.rsqrt(ms + EPS) * g


def _sigmoid(x):
    return 0.5 * jnp.tanh(0.5 * x) + 0.5


def _gelu(x):
    hx = 0.5 * x
    return hx + hx * jnp.tanh(x * (GELU_C + (GELU_C * 0.044715) * (x * x)))


def _lam(lamv_ref):
    lv = lamv_ref[...]
    s1 = jnp.sum(lv[0:1] * lv[1:2], axis=-1, keepdims=True)
    s2 = jnp.sum(lv[2:3] * lv[3:4], axis=-1, keepdims=True)
    return jnp.exp(s1) - jnp.exp(s2) + LAM_INIT


IN_PROJ_TM = 256
IN_PROJ_TN = 512
GN_W = 256


def _in_proj_kernel(x_ref, g1_ref, w_ref, qg_ref, kg_ref, gm_ref,
                    q_ref, kb_ref, k_ref, vb_ref, v_ref, xr_ref, gr_ref, ga_ref, gl_ref):
    h = _rms(x_ref[...], g1_ref[...]).astype(BF16)

    def proj(c0, c1):
        return jnp.dot(h, w_ref[:, c0:c1], preferred_element_type=F32)

    def group_norm(z, g):
        outs = []
        gw = gm_ref.shape[0]
        for j in range(z.shape[1] // gw):
            zj = z[:, j * gw:(j + 1) * gw]
            ms = jnp.dot((zj * zj).astype(BF16), gm_ref[...], preferred_element_type=F32)
            outs.append(zj * lax.rsqrt(ms + EPS) * g)
        return jnp.concatenate(outs, axis=1)

    tn = IN_PROJ_TN
    for c in range(0, QKV_W, tn):
        q = group_norm(proj(OFF_Q + c, OFF_Q + c + tn), qg_ref[...])
        q_ref[:, c:c + tn] = q.astype(BF16)
        k = group_norm(proj(OFF_K + c, OFF_K + c + tn), kg_ref[...])
        k_ref[:, c:c + tn] = k
        kb_ref[:, c:c + tn] = k.astype(BF16)
        v = proj(OFF_V + c, OFF_V + c + tn)
        v_ref[:, c:c + tn] = v
        vb_ref[:, c:c + tn] = v.astype(BF16)
        ga_ref[:, c:c + tn] = proj(OFF_GA + c, OFF_GA + c + tn)
        gl_ref[:, c:c + tn] = proj(OFF_GL + c, OFF_GL + c + tn)
    for c in range(0, LRU_W, tn):
        c1 = min(c + tn, LRU_W)
        xr_ref[:, c:c1] = proj(OFF_XR + c, OFF_XR + c1)
        gr_ref[:, c:c1] = proj(OFF_GR + c, OFF_GR + c1)


def _in_proj(x2d, g1, w_in_bf, qg, kg, gmat):
    t = x2d.shape[0]
    tm = IN_PROJ_TM
    row = lambda w: pl.BlockSpec((tm, w), lambda i: (i, 0))
    out_shapes = [
        jax.ShapeDtypeStruct((t, QKV_W), BF16),
        jax.ShapeDtypeStruct((t, QKV_W), BF16),
        jax.ShapeDtypeStruct((t, QKV_W), F32),
        jax.ShapeDtypeStruct((t, QKV_W), BF16),
        jax.ShapeDtypeStruct((t, QKV_W), F32),
        jax.ShapeDtypeStruct((t, LRU_W), F32),
        jax.ShapeDtypeStruct((t, LRU_W), F32),
        jax.ShapeDtypeStruct((t, D_MODEL), F32),
        jax.ShapeDtypeStruct((t, D_MODEL), F32),
    ]
    return pl.pallas_call(
        _in_proj_kernel,
        grid=(t // tm,),
        in_specs=[row(D_MODEL), _resident((1, D_MODEL)), _resident((D_MODEL, IN_W)),
                  _resident((1, GN_W)), _resident((1, GN_W)), _resident((GN_W, GN_W))],
        out_specs=[row(s.shape[1]) for s in out_shapes],
        out_shape=out_shapes,
        compiler_params=pltpu.CompilerParams(dimension_semantics=("arbitrary",),
                                             vmem_limit_bytes=VMEM_LIMIT),
        name="in_proj",
    )(x2d, g1, w_in_bf, qg, kg, gmat)


ATT_BLK = 512
ATT_CHAIN = 256


def _prompt_attn_kernel(q_ref, k_ref, v_ref, slope_ref, lamv_ref, sg_ref, o_ref,
                        q2_s, vx_s, m_s, acc_s):
    blk = ATT_BLK
    qi = pl.program_id(2)

    @pl.when(qi == 0)
    def _():
        vx_s[:, :DV] = v_ref[0]
        vx_s[:, DV:] = jnp.ones((vx_s.shape[0], LANES), BF16)

    q = q_ref[0]
    lane = lax.broadcasted_iota(jnp.int32, q.shape, 1)
    q2_s[:blk, :] = jnp.where(lane < DK, q, jnp.zeros_like(q))
    q2_s[blk:, :] = jnp.where(lane >= DK, q, jnp.zeros_like(q))
    m_s[...] = jnp.full(m_s.shape, NEG, F32)
    acc_s[...] = jnp.zeros(acc_s.shape, F32)

    slope = slope_ref[0][:, :1]

    def block(ki, nblk, masked):
        width = nblk * blk
        start = pl.multiple_of(ki * blk, blk)
        k = k_ref[0, pl.ds(start, width), :]
        vx = vx_s[pl.ds(start, width), :]
        col_f = lax.broadcasted_iota(jnp.int32, (1, width), 1).astype(F32)
        bias = slope * (col_f + ((ki - qi) * blk).astype(F32))
        if masked:
            row = lax.broadcasted_iota(jnp.int32, (blk, blk), 0)
            col = lax.broadcasted_iota(jnp.int32, (blk, blk), 1)
            keep = col <= row
        halves = tuple(slice(r, r + ATT_CHAIN) for r in range(0, 2 * blk, ATT_CHAIN))
        scores = []
        for n, rows in enumerate(halves):
            s = lax.dot_general(q2_s[rows, :], k, NT_DIMS, preferred_element_type=F32) + bias
            if masked:
                r0 = (n * ATT_CHAIN) % blk
                s = jnp.where(keep[r0:r0 + ATT_CHAIN], s, NEG)
            scores.append(s)
        probs = []
        for rows, s in zip(halves, scores):
            m_prev = m_s[rows, :]
            m_next = jnp.maximum(m_prev, jnp.max(s, axis=-1, keepdims=True))
            probs.append((jnp.exp2(s - jnp.tile(m_next, (1, width // LANES))).astype(BF16),
                          jnp.exp2(m_prev - m_next)))
            m_s[rows, :] = m_next
        for rows, (p, alpha) in zip(halves, probs):
            pv = jnp.dot(p, vx, preferred_element_type=F32)
            acc_s[rows, :] = jnp.tile(alpha, (1, 2)) * acc_s[rows, :] + pv

    def body(j, carry):
        block(2 * j, 2, False)
        return carry

    lax.fori_loop(0, qi // 2, body, 0)

    @pl.when(qi % 2 == 1)
    def _():
        block(qi - 1, 1, False)

    block(qi, 1, True)

    lam = _lam(lamv_ref)
    acc = acc_s[...]
    o1 = acc[:blk, :DV] / acc[:blk, DV:]
    o2 = acc[blk:, :DV] / acc[blk:, DV:]
    o_ref[0] = (_rms(o1 - lam * o2, sg_ref[...]) * (1.0 - LAM_INIT)).astype(BF16)


def _prompt_attn(q, k, v, slopes, lamv, sg):
    b, s, _ = q.shape
    blk = ATT_BLK
    seq = pl.BlockSpec((1, s, LANES), lambda bb, h, i: (bb, 0, h))
    tile = pl.BlockSpec((1, blk, LANES), lambda bb, h, i: (bb, i, h))
    return pl.pallas_call(
        _prompt_attn_kernel,
        grid=(b, N_HEADS, s // blk),
        in_specs=[tile, seq, seq,
                  pl.BlockSpec((1, 1, LANES), lambda bb, h, i: (h, 0, 0)),
                  pl.BlockSpec((4, DK), lambda bb, h, i: (0, 0)),
                  pl.BlockSpec((1, DV), lambda bb, h, i: (0, 0))],
        out_specs=tile,
        out_shape=jax.ShapeDtypeStruct((b, s, ATTN_W), BF16),
        scratch_shapes=[pltpu.VMEM((2 * blk, LANES), BF16), pltpu.VMEM((s, 2 * LANES), BF16),
                        pltpu.VMEM((2 * blk, LANES), F32), pltpu.VMEM((2 * blk, 2 * LANES), F32)],
        compiler_params=pltpu.CompilerParams(
            dimension_semantics=("arbitrary", "arbitrary", "arbitrary"),
            vmem_limit_bytes=VMEM_LIMIT),
        name="prompt_attn",
    )(q, k, v, slopes, lamv, sg)


DEC_P = 16
PAIR_ROWS = 2 * PAGE_SIZE
QP_ROWS = 32
N_PAIRS = N_HEADS // 2


def _decode_kernel(pt_ref, q_ref, kn_ref, vn_ref, bias_ref, biasn_ref, slope_ref, lamv_ref, sg_ref,
                   *rest, n_chunks):
    del pt_ref
    kp = rest[:DEC_P]
    vp = rest[DEC_P:2 * DEC_P]
    o_ref = rest[2 * DEC_P]
    qs, m_s, l_s, acc_s = rest[2 * DEC_P + 1:]
    c = pl.program_id(1)
    t_new = q_ref.shape[1]

    @pl.when(c == 0)
    def _():
        q = q_ref[0]
        lane = lax.broadcasted_iota(jnp.int32, (t_new, LANES), 1)
        for p in range(N_PAIRS):
            parts = []
            for hs in range(2):
                h = p + N_PAIRS * hs
                qh = q[:, h * LANES:(h + 1) * LANES]
                parts.append(jnp.where(lane < DK, qh, 0.0))
                parts.append(jnp.where(lane >= DK, qh, 0.0))
            qs[p] = jnp.concatenate(parts, axis=0).astype(BF16)
        m_s[...] = jnp.full(m_s.shape, NEG, F32)
        l_s[...] = jnp.zeros(l_s.shape, F32)
        acc_s[...] = jnp.zeros(acc_s.shape, F32)

    def pair_rows(refs, p, n):
        return jnp.concatenate([r[0, pl.ds(p, n, stride=N_PAIRS), :] for r in refs], axis=0).astype(BF16)

    def softmax_update(p, u, shift_back):
        m_old = m_s[p]
        m_new = jnp.maximum(m_old, jnp.max(u, axis=-1, keepdims=True) + shift_back)
        ref = m_new - shift_back
        n = u.shape[1]
        pe = jnp.exp2(u - (jnp.tile(ref, (1, n // LANES)) if n >= LANES else ref[:, :n]))
        alpha = jnp.exp2(m_old - m_new)
        l_s[p] = alpha * l_s[p] + jnp.sum(pe, axis=-1, keepdims=True)
        m_s[p] = m_new
        return pe.astype(BF16), alpha

    chunk_off = ((c + 1 - n_chunks) * (DEC_P * PAGE_SIZE)).astype(F32)
    scores = [lax.dot_general(qs[p], pair_rows(kp, p, PAIR_ROWS), NT_DIMS, preferred_element_type=F32)
              + bias_ref[p] for p in range(N_PAIRS)]
    probs = [softmax_update(p, scores[p], slope_ref[p] * chunk_off) for p in range(N_PAIRS)]
    for p in range(N_PAIRS):
        pb, alpha = probs[p]
        acc_s[p] = alpha * acc_s[p] + jnp.dot(pb, pair_rows(vp, p, PAIR_ROWS), preferred_element_type=F32)

    @pl.when(c == n_chunks - 1)
    def _():
        lam = _lam(lamv_ref)
        for p in range(N_PAIRS):
            kn = pair_rows([kn_ref], p, 2 * t_new)
            vn = pair_rows([vn_ref], p, 2 * t_new)
            u = lax.dot_general(qs[p], kn, NT_DIMS, preferred_element_type=F32) + biasn_ref[p]
            pb, alpha = softmax_update(p, u, 0.0)
            acc = alpha * acc_s[p] + jnp.dot(pb, vn, preferred_element_type=F32)
            out = acc / l_s[p]
            for hs in range(2):
                h = p + N_PAIRS * hs
                r0 = hs * 2 * t_new
                a = out[r0:r0 + t_new] - lam * out[r0 + t_new:r0 + 2 * t_new]
                o_ref[0, :, h * LANES:(h + 1) * LANES] = _rms(a, sg_ref[...]) * (1.0 - LAM_INIT)


def _decode_tables(t_new):
    slopes = LOG2E * 2.0 ** (-8.0 * np.arange(1, N_HEADS + 1) / N_HEADS)
    rows = np.arange(QP_ROWS)
    row_hs, row_t = rows // (2 * t_new), rows % t_new
    lane = np.arange(PAIR_ROWS * DEC_P)
    lane_hs = lane % 2
    lane_key = (lane % PAIR_ROWS) // 2 + (lane // PAIR_ROWS) * PAGE_SIZE
    lane_n = np.arange(2 * t_new)
    bias = np.zeros((N_PAIRS, QP_ROWS, lane.size), np.float32)
    biasn = np.zeros((N_PAIRS, QP_ROWS, lane_n.size), np.float32)
    slope_rows = np.zeros((N_PAIRS, QP_ROWS, LANES), np.float32)
    for p in range(N_PAIRS):
        sl = slopes[p + N_PAIRS * row_hs]
        slope_rows[p] = sl[:, None]
        match = row_hs[:, None] == lane_hs[None, :]
        rel = lane_key[None, :] - DEC_P * PAGE_SIZE
        bias[p] = np.where(match, sl[:, None] * rel, NEG)
        match_n = (row_hs[:, None] == (lane_n % 2)[None, :]) & ((lane_n // 2)[None, :] <= row_t[:, None])
        biasn[p] = np.where(match_n, sl[:, None] * (lane_n // 2)[None, :], NEG)
    return jnp.asarray(bias), jnp.asarray(biasn), jnp.asarray(slope_rows)


def _decode_attn(page_table, q, kn, vn, cache_k, cache_v, lamv, sg):
    bd, t_new, _ = q.shape
    n_pages = page_table.shape[1]
    n_chunks = n_pages // DEC_P
    n_pool = cache_k.shape[0]
    bias, biasn, slope_rows = _decode_tables(t_new)
    page_rows = PAGE_SIZE * N_HEADS

    def page_spec(i):
        return pl.BlockSpec((1, page_rows, LANES), lambda b, c, pt, i=i: (pt[b, c * DEC_P + i], 0, 0))

    const = lambda shape: pl.BlockSpec(shape, lambda b, c, pt: (0,) * len(shape))
    per_b = lambda shape: pl.BlockSpec(shape, lambda b, c, pt: (b,) + (0,) * (len(shape) - 1))
    grid_spec = pltpu.PrefetchScalarGridSpec(
        num_scalar_prefetch=1,
        grid=(bd, n_chunks),
        in_specs=[per_b((1, t_new, ATTN_W)), per_b((1, t_new * N_HEADS, LANES)),
                  per_b((1, t_new * N_HEADS, LANES)),
                  const(bias.shape), const(biasn.shape), const(slope_rows.shape),
                  const((4, DK)), const((1, DV))]
                 + [page_spec(i) for i in range(DEC_P)] + [page_spec(i) for i in range(DEC_P)],
        out_specs=per_b((1, t_new, ATTN_W)),
        scratch_shapes=[pltpu.VMEM((N_PAIRS, QP_ROWS, LANES), BF16), pltpu.VMEM((N_PAIRS, QP_ROWS, LANES), F32),
                        pltpu.VMEM((N_PAIRS, QP_ROWS, LANES), F32), pltpu.VMEM((N_PAIRS, QP_ROWS, DV), F32)],
    )
    ck = cache_k.reshape(n_pool, page_rows, LANES)
    cv = cache_v.reshape(n_pool, page_rows, LANES)
    return pl.pallas_call(
        functools.partial(_decode_kernel, n_chunks=n_chunks),
        grid_spec=grid_spec,
        out_shape=jax.ShapeDtypeStruct((bd, t_new, ATTN_W), F32),
        compiler_params=pltpu.CompilerParams(dimension_semantics=("arbitrary", "arbitrary"),
                                             vmem_limit_bytes=VMEM_LIMIT),
        name="decode_attn",
    )(page_table, q, kn, vn, bias, biasn, slope_rows, lamv, sg,
      *([ck] * DEC_P), *([cv] * DEC_P))


def _band(c0, c1):
    lo = (c0 // LRU_BW) * LRU_BW
    hi = ((c1 - 1) // LRU_BW + 1) * LRU_BW
    return lo // LANES * LANES, min(_round_up(hi, LANES), LRU_W)


def _lru_kernel(xr_ref, gr_ref, cs_ref, h0_ref, cw_ref, cb_ref, wa_ref, ba_ref, wx_ref, bx_ref, lam_ref,
                yl_ref, hl_ref, xp_s, h_s, a_s, gx_s, hs_s, *, rps, pad):
    tr = xr_ref.shape[1]
    hist = (LRU_CONV - 1) * rps

    @pl.when(pl.program_id(1) == 0)
    def _():
        xp_s[pad - hist:pad, :] = cs_ref[0]
        h_s[...] = h0_ref[0]

    xp_s[pad:pad + tr, :] = xr_ref[0]
    xc = cb_ref[...]
    for j in range(LRU_CONV):
        off = pad - (LRU_CONV - 1 - j) * rps
        xc = xc + cw_ref[j:j + 1, :] * xp_s[off:off + tr, :]
    xcb = xc.astype(BF16)

    def gate(w_ref, b_ref):
        outs = []
        for c0 in range(0, LRU_W, 2 * LANES):
            c1 = min(c0 + 2 * LANES, LRU_W)
            lo, hi = _band(c0, c1)
            outs.append(jnp.dot(xcb[:, lo:hi], w_ref[lo:hi, c0:c1], preferred_element_type=F32))
        return _sigmoid(jnp.concatenate(outs, axis=1) + b_ref[...])

    r = gate(wa_ref, ba_ref)
    i = gate(wx_ref, bx_ref)
    z = -lam_ref[...]
    softplus = jnp.maximum(z, 0.0) + jnp.log1p(jnp.exp(-jnp.abs(z)))
    log_a = (-LRU_C * r) * softplus
    a = jnp.exp(log_a)
    a_s[...] = a
    gx_s[...] = jnp.sqrt(-jnp.tanh(log_a) * (a * a + 1.0)) * (i * xc)

    def step(t, h):
        rows = pl.ds(pl.multiple_of(t * rps, rps), rps)
        h = a_s[rows, :] * h + gx_s[rows, :]
        hs_s[rows, :] = h
        return h

    h = lax.fori_loop(0, tr // rps, step, h_s[...], unroll=8)
    h_s[...] = h
    hl_ref[0] = h
    yl_ref[0] = (hs_s[...] * _gelu(gr_ref[0])).astype(BF16)
    xp_s[pad - hist:pad, :] = xp_s[pad + tr - hist:pad + tr, :]


def _lru(xr, gr, conv_state, h0, cw, cb, wa, ba, wx, bx, lam, *, rps, tr):
    nb, rows, _ = xr.shape
    pad = _round_up((LRU_CONV - 1) * rps, SUBLANES)
    tile = pl.BlockSpec((1, tr, LRU_W), lambda b, s: (b, s, 0))
    per_b = lambda shape: pl.BlockSpec(shape, lambda b, s: (b, 0, 0))
    return pl.pallas_call(
        functools.partial(_lru_kernel, rps=rps, pad=pad),
        grid=(nb, rows // tr),
        in_specs=[tile, tile, per_b((1, (LRU_CONV - 1) * rps, LRU_W)), per_b((1, rps, LRU_W)),
                  _resident((LRU_CONV, LRU_W)), _resident((1, LRU_W)),
                  _resident((LRU_W, LRU_W)), _resident((1, LRU_W)),
                  _resident((LRU_W, LRU_W)), _resident((1, LRU_W)), _resident((1, LRU_W))],
        out_specs=[tile, per_b((1, rps, LRU_W))],
        out_shape=[jax.ShapeDtypeStruct((nb, rows, LRU_W), BF16),
                   jax.ShapeDtypeStruct((nb, rps, LRU_W), F32)],
        scratch_shapes=[pltpu.VMEM((pad + tr, LRU_W), F32), pltpu.VMEM((rps, LRU_W), F32),
                        pltpu.VMEM((tr, LRU_W), F32), pltpu.VMEM((tr, LRU_W), F32),
                        pltpu.VMEM((tr, LRU_W), F32)],
        compiler_params=pltpu.CompilerParams(dimension_semantics=("arbitrary", "arbitrary"),
                                             vmem_limit_bytes=VMEM_LIMIT),
        name="rg_lru",
    )(xr, gr, conv_state, h0, cw, cb, wa, ba, wx, bx, lam)


FFN_TC = 512


def _post_kernel(x_ref, o_ref, yl_ref, ga_ref, gl_ref, fs_ref,
                 wba_ref, wbl_ref, wo_ref, g2_ref, wup_ref, fw_ref, fb_ref, wdn_ref,
                 y_ref, fso_ref, tail_s, xp_s, *, rps, pad):
    tr = x_ref.shape[1]
    hist = (FFN_CONV - 1) * rps

    @pl.when(pl.program_id(1) == 0)
    def _():
        tail_s[pad - hist:pad, :] = fs_ref[0]

    ya = jnp.dot(o_ref[0], wba_ref[...], preferred_element_type=F32)
    yr = jnp.dot(yl_ref[0], wbl_ref[...], preferred_element_type=F32)
    mix = _sigmoid(ga_ref[0]) * ya + _sigmoid(gl_ref[0]) * yr
    x1 = x_ref[0] + jnp.dot(mix.astype(BF16), wo_ref[...], preferred_element_type=F32)
    h2 = _rms(x1, g2_ref[...]).astype(BF16)

    def up_proj(c):
        return [jnp.dot(h2, wup_ref[:, off:off + FFN_TC], preferred_element_type=F32) for off in (c, FF + c)]

    def conv_cols(up, off, buf):
        buf[pad - hist:pad, :] = tail_s[pad - hist:pad, off:off + FFN_TC]
        buf[pad:pad + tr, :] = up
        tail_s[pad - hist:pad, off:off + FFN_TC] = up[tr - hist:tr, :]
        out = fb_ref[:, off:off + FFN_TC]
        for j in range(FFN_CONV):
            r0 = pad - (FFN_CONV - 1 - j) * rps
            out = out + fw_ref[j:j + 1, off:off + FFN_TC] * buf[r0:r0 + tr, :]
        return out

    acc = x1
    chunks = list(range(0, FF, FFN_TC))
    ups_next = up_proj(chunks[0])
    for n, c in enumerate(chunks):
        ups = ups_next
        if n + 1 < len(chunks):
            ups_next = up_proj(chunks[n + 1])
        u = conv_cols(ups[0], c, xp_s.at[2 * n])
        g = conv_cols(ups[1], FF + c, xp_s.at[2 * n + 1])
        act = (_gelu(g) * u).astype(BF16)
        acc = acc + jnp.dot(act, wdn_ref[c:c + FFN_TC, :], preferred_element_type=F32)
    y_ref[0] = acc
    fso_ref[0] = tail_s[pad - hist:pad, :]


def _post(x, o, yl, ga, gl, ffn_state, wba, wbl, wo, g2, wup, fw, fb, wdn, *, rps, tr):
    nb, rows, _ = x.shape
    pad = _round_up((FFN_CONV - 1) * rps, SUBLANES)
    tile = lambda w: pl.BlockSpec((1, tr, w), lambda b, s: (b, s, 0))
    state = pl.BlockSpec((1, (FFN_CONV - 1) * rps, 2 * FF), lambda b, s: (b, 0, 0))
    return pl.pallas_call(
        functools.partial(_post_kernel, rps=rps, pad=pad),
        grid=(nb, rows // tr),
        in_specs=[tile(D_MODEL), tile(ATTN_W), tile(LRU_W), tile(D_MODEL), tile(D_MODEL), state,
                  _resident((ATTN_W, D_MODEL)), _resident((LRU_W, D_MODEL)), _resident((D_MODEL, D_MODEL)),
                  _resident((1, D_MODEL)), _resident((D_MODEL, 2 * FF)), _resident((FFN_CONV, 2 * FF)),
                  _resident((1, 2 * FF)), _resident((FF, D_MODEL))],
        out_specs=[tile(D_MODEL), state],
        out_shape=[jax.ShapeDtypeStruct((nb, rows, D_MODEL), F32),
                   jax.ShapeDtypeStruct((nb, (FFN_CONV - 1) * rps, 2 * FF), F32)],
        scratch_shapes=[pltpu.VMEM((pad, 2 * FF), F32),
                        pltpu.VMEM((2 * FF // FFN_TC, pad + tr, FFN_TC), F32)],
        compiler_params=pltpu.CompilerParams(dimension_semantics=("arbitrary", "arbitrary"),
                                             vmem_limit_bytes=VMEM_LIMIT),
        name="out_ffn",
    )(x, o, yl, ga, gl, ffn_state, wba, wbl, wo, g2, wup, fw, fb, wdn)


def _block_diag(w):
    n, bw, _ = w.shape
    width = n * bw
    blocks = [jnp.pad(w[i], ((i * bw, width - (i + 1) * bw),) * 2) for i in range(n)]
    return functools.reduce(jnp.add, blocks)


SAMPLE_GROUPS = 4


def _time_major(a):
    b, t, c = a.shape
    g = SAMPLE_GROUPS
    return jnp.swapaxes(a.reshape(g, b // g, t, c), 1, 2).reshape(g, t * (b // g), c)


def _batch_major(a, t):
    g, rows, c = a.shape
    return jnp.swapaxes(a.reshape(g, t, rows // t, c), 1, 2).reshape(g * (rows // t), t, c)


def kernel(x_prompt, x_sample, cache_k, cache_v, page_table, state_lru_h, state_lru_conv, state_ffn_conv, norm1_g, w_in, q_norm_g, k_norm_g, lambda_q1, lambda_k1, lambda_q2, lambda_k2, subln_g, lru_conv_w, lru_conv_b, lru_wa, lru_ba, lru_wx, lru_bx, lru_lambda, w_branch_attn, w_branch_lru, w_out, norm2_g, w_up, ffn_conv_w, ffn_conv_b, w_down):
    assert w_in.shape[0] == 1, "single-layer step"
    b, s, _ = x_prompt.shape
    bd, t_new, _ = x_sample.shape
    row = lambda a: a.reshape(1, -1)

    w_in_bf = w_in[0].astype(BF16)
    g1 = row(norm1_g[0])
    qg = row(jnp.tile(q_norm_g[0], GN_W // DK)) * (DK ** -0.5 * LOG2E)
    kg = row(jnp.tile(k_norm_g[0], GN_W // DK))
    gid = np.arange(GN_W) // DK
    gmat = jnp.asarray((gid[:, None] == gid[None, :]).astype(np.float32) / DK, BF16)
    lamv = jnp.stack([lambda_q1[0], lambda_k1[0], lambda_q2[0], lambda_k2[0]]).astype(F32)
    sg = row(subln_g[0])
    slopes = jnp.asarray(np.broadcast_to(
        (LOG2E * 2.0 ** (-8.0 * np.arange(1, N_HEADS + 1) / N_HEADS)).astype(np.float32)[:, None, None],
        (N_HEADS, 1, LANES)))
    wa = _block_diag(lru_wa[0]).astype(BF16)
    wx = _block_diag(lru_wx[0]).astype(BF16)
    lru_params = (lru_conv_w[0], row(lru_conv_b[0]), wa, row(lru_ba[0]), wx, row(lru_bx[0]),
                  row(lru_lambda[0]))
    post_params = (w_branch_attn[0].astype(BF16), w_branch_lru[0].astype(BF16), w_out[0].astype(BF16),
                   row(norm2_g[0]), w_up[0].astype(BF16), ffn_conv_w[0], row(ffn_conv_b[0]),
                   w_down[0].astype(BF16))

    qp, kbp, kp, vbp, vp, xrp, grp, gap, glp = _in_proj(
        x_prompt.reshape(b * s, D_MODEL), g1, w_in_bf, qg, kg, gmat)
    r3 = lambda a: a.reshape(b, s, a.shape[-1])
    o_p = _prompt_attn(r3(qp), r3(kbp), r3(vbp), slopes, lamv, sg)
    xrp3 = r3(xrp)
    yl_p, h_p = _lru(xrp3, r3(grp), jnp.zeros((b, LRU_CONV - 1, LRU_W), F32), jnp.zeros((b, 1, LRU_W), F32),
                     *lru_params, rps=1, tr=256)
    y_p, fs_p = _post(x_prompt, o_p, yl_p, r3(gap), r3(glp), jnp.zeros((b, FFN_CONV - 1, 2 * FF), F32),
                      *post_params, rps=1, tr=256)

    qs_, kbs, ks, vbs, vs, xrs, grs, gas, gls = _in_proj(
        x_sample.reshape(bd * t_new, D_MODEL), g1, w_in_bf, qg, kg, gmat)
    del kbs, vbs
    o_s = _decode_attn(page_table, qs_.astype(F32).reshape(bd, t_new, ATTN_W),
                       ks.reshape(bd, t_new * N_HEADS, LANES), vs.reshape(bd, t_new * N_HEADS, LANES),
                       cache_k[0], cache_v[0], lamv, sg)
    d3 = lambda a: a.reshape(bd, t_new, a.shape[-1])
    bg = bd // SAMPLE_GROUPS
    xrs_t = _time_major(d3(xrs))
    yl_s, h_s = _lru(xrs_t, _time_major(d3(grs)), _time_major(state_lru_conv[0]),
                     state_lru_h[0].reshape(SAMPLE_GROUPS, bg, LRU_W), *lru_params, rps=bg, tr=bg * t_new)
    y_s, fs_s = _post(_time_major(x_sample), _time_major(o_s.astype(BF16)), yl_s,
                      _time_major(d3(gas)), _time_major(d3(gls)),
                      _time_major(state_ffn_conv[0]), *post_params, rps=bg, tr=bg * t_new)

    hd = lambda a, n, t: a.reshape(1, n, t, N_HEADS, LANES)
    return (y_p, _batch_major(y_s, t_new),
            hd(kp, b, s), hd(vp, b, s),
            h_p.reshape(1, b, LRU_W), xrp3[None, :, s - (LRU_CONV - 1):, :], fs_p[None],
            hd(ks, bd, t_new), hd(vs, bd, t_new),
            h_s.reshape(1, bd, LRU_W),
            _batch_major(xrs_t[:, (t_new - LRU_CONV + 1) * bg:, :], LRU_CONV - 1)[None],
            _batch_major(fs_s, FFN_CONV - 1)[None])
```
